```python
import jax, jax.numpy as jnp
from jax import lax
import numpy as np

D_MODEL = 2048
BATCH = 1
SEQ = 16384
DEPTH = 1

D_CONV = D_MODEL // 2
CONV_WIDTH = 3
N_HEADS = 16
N_KV_HEADS = 4
HEAD_DIM = 64
D_ATTN = N_HEADS * HEAD_DIM
D_KV = N_KV_HEADS * HEAD_DIM
WINDOW = 128
BLOCK = 128
ROPE_THETA = 10000.0
N_BRANCH = 2
D_FF = 5632
EPS = 1e-6

SPLIT_SIZES = (D_CONV, D_CONV, D_CONV, D_ATTN, D_KV, D_KV, D_MODEL, D_MODEL)
SPLIT_POINTS = tuple(int(p) for p in np.cumsum(SPLIT_SIZES)[:-1])
D_IN_PROJ = int(sum(SPLIT_SIZES))

kernel_name = "hybrid_gated_shortconv_swa_convffn"


def rmsnorm(x, g):
    xf = x.astype(jnp.float32)
    y = xf * lax.rsqrt(jnp.mean(xf * xf, axis=-1, keepdims=True) + EPS)
    return (y * g.astype(jnp.float32)).astype(x.dtype)


def centred_dwconv3(x, w):
    xp = jnp.pad(x, ((0, 0), (1, 1), (0, 0)))
    return xp[:, :-2] * w[0] + xp[:, 1:-1] * w[1] + xp[:, 2:] * w[2]


def rope(x, seq_len):
    half = HEAD_DIM // 2
    inv_freq = ROPE_THETA ** (-jnp.arange(0, half, dtype=jnp.float32) / half)
    ang = jnp.arange(seq_len, dtype=jnp.float32)[:, None] * inv_freq[None, :]
    cos = jnp.cos(ang)[None, :, None, :]
    sin = jnp.sin(ang)[None, :, None, :]
    xf = x.astype(jnp.float32)
    x1, x2 = xf[..., :half], xf[..., half:]
    out = jnp.concatenate([x1 * cos - x2 * sin, x2 * cos + x1 * sin], axis=-1)
    return out.astype(x.dtype)


def banded_window_attention(q, k, v, sink):
    b, s = q.shape[0], q.shape[1]
    nb = s // BLOCK
    grp = N_HEADS // N_KV_HEADS
    qb = q.reshape(b, nb, BLOCK, N_KV_HEADS, grp, HEAD_DIM)
    pad = ((0, 0), (BLOCK, BLOCK), (0, 0), (0, 0))
    kp = jnp.pad(k, pad).reshape(b, nb + 2, BLOCK, N_KV_HEADS, HEAD_DIM)
    vp = jnp.pad(v, pad).reshape(b, nb + 2, BLOCK, N_KV_HEADS, HEAD_DIM)
    kb = jnp.concatenate([kp[:, :-2], kp[:, 1:-1], kp[:, 2:]], axis=2)
    vb = jnp.concatenate([vp[:, :-2], vp[:, 1:-1], vp[:, 2:]], axis=2)
    scale = HEAD_DIM ** -0.5
    scores = jnp.einsum('bnqkgd,bnskd->bnkgqs', qb, kb).astype(jnp.float32) * scale
    blk = jnp.arange(nb)[:, None, None]
    qpos = blk * BLOCK + jnp.arange(BLOCK)[None, :, None]
    kpos = (blk - 1) * BLOCK + jnp.arange(3 * BLOCK)[None, None, :]
    mask = (jnp.abs(kpos - qpos) <= WINDOW) & (kpos >= 0) & (kpos < s)
    scores = jnp.where(mask[None, :, None, None], scores, -jnp.inf)
    sk = sink.astype(jnp.float32).reshape(1, 1, N_KV_HEADS, grp, 1)
    m = jnp.maximum(jnp.max(scores, axis=-1), sk)
    p = jnp.exp(scores - m[..., None])
    denom = jnp.sum(p, axis=-1) + jnp.exp(sk - m)
    p = (p / denom[..., None]).astype(v.dtype)
    out = jnp.einsum('bnkgqs,bnskd->bnqkgd', p, vb)
    return out.reshape(b, s, N_HEADS, HEAD_DIM)


def setup_inputs(seed: int = 0) -> dict:
    key = jax.random.key(seed)
    ks = jax.random.split(key, 16)
    f32 = jnp.float32

    def nrm(k, shape, fan_in):
        return jax.random.normal(k, shape, f32) * (fan_in ** -0.5)

    x = jax.random.normal(ks[0], (BATCH, SEQ, D_MODEL), f32)
    return {
        "x": x,
        "norm_mix_g": 1.0 + 0.02 * jax.random.normal(ks[1], (DEPTH, D_MODEL), f32),
        "w_in": nrm(ks[2], (DEPTH, D_MODEL, D_IN_PROJ), D_MODEL),
        "b_gate": 0.02 * jax.random.normal(ks[3], (DEPTH, N_BRANCH * D_MODEL), f32),
        "conv_a_w": nrm(ks[4], (DEPTH, CONV_WIDTH, D_CONV), CONV_WIDTH),
        "w_out_a": nrm(ks[5], (DEPTH, D_CONV, D_MODEL), D_CONV),
        "sink_logits": 0.5 * jax.random.normal(ks[6], (DEPTH, N_HEADS), f32),
        "w_o_attn": nrm(ks[7], (DEPTH, D_ATTN, D_MODEL), D_ATTN),
        "w_mix_out": nrm(ks[8], (DEPTH, D_MODEL, D_MODEL), D_MODEL),
        "norm_ffn_g": 1.0 + 0.02 * jax.random.normal(ks[9], (DEPTH, D_MODEL), f32),
        "ffn_w_up": nrm(ks[10], (DEPTH, D_MODEL, 2 * D_FF), D_MODEL),
        "ffn_conv_w": nrm(ks[11], (DEPTH, CONV_WIDTH, 2 * D_FF), CONV_WIDTH),
        "ffn_conv_b": 0.02 * jax.random.normal(ks[12], (DEPTH, 2 * D_FF), f32),
        "ffn_w_down": nrm(ks[13], (DEPTH, D_FF, D_MODEL), D_FF),
        "norm_final_g": 1.0 + 0.02 * jax.random.normal(ks[14], (D_MODEL,), f32),
    }


def reference(x, norm_mix_g, w_in, b_gate, conv_a_w, w_out_a, sink_logits, w_o_attn,
              w_mix_out, norm_ffn_g, ffn_w_up, ffn_conv_w, ffn_conv_b, ffn_w_down,
              norm_final_g):
    b, s, _ = x.shape
    h = x
    for l in range(DEPTH):
        u = rmsnorm(h, norm_mix_g[l])
        z = u @ w_in[l]
        b_a, c_a, v_a, q, k, v, gl_a, gl_b = jnp.split(z, SPLIT_POINTS, axis=-1)
        y_a = (b_a * centred_dwconv3(c_a * v_a, conv_a_w[l])) @ w_out_a[l]
        q = rope(q.reshape(b, s, N_HEADS, HEAD_DIM), s)
        k = rope(k.reshape(b, s, N_KV_HEADS, HEAD_DIM), s)
        v = v.reshape(b, s, N_KV_HEADS, HEAD_DIM)
        att = banded_window_attention(q, k, v, sink_logits[l])
        y_b = att.reshape(b, s, D_ATTN) @ w_o_attn[l]
        g_a = jax.nn.sigmoid(gl_a + b_gate[l, :D_MODEL])
        g_b = jax.nn.sigmoid(gl_b + b_gate[l, D_MODEL:])
        h = h + (g_a * y_a + g_b * y_b) @ w_mix_out[l]
        u2 = rmsnorm(h, norm_ffn_g[l])
        up = centred_dwconv3(u2 @ ffn_w_up[l], ffn_conv_w[l]) + ffn_conv_b[l]
        a, gv = up[..., :D_FF], up[..., D_FF:]
        h = h + (jax.nn.silu(a) * gv) @ ffn_w_down[l]
    return rmsnorm(h, norm_final_g)
```

```python
import functools

import jax
import jax.numpy as jnp
from jax import lax
from jax.experimental import pallas as pl
from jax.experimental.pallas import tpu as pltpu

D_MODEL = 2048
D_CONV = D_MODEL // 2
N_HEADS = 16
N_KV_HEADS = 4
HEAD_DIM = 64
D_ATTN = N_HEADS * HEAD_DIM
D_KV = N_KV_HEADS * HEAD_DIM
WINDOW = 128
BLOCK = 128
ROPE_THETA = 10000.0
D_FF = 5632
EPS = 1e-6

OFF_B = 0
OFF_C = OFF_B + D_CONV
OFF_VA = OFF_C + D_CONV
OFF_Q = OFF_VA + D_CONV
OFF_K = OFF_Q + D_ATTN
OFF_V = OFF_K + D_KV
OFF_GA = OFF_V + D_KV
OFF_GB = OFF_GA + D_MODEL
D_IN_PROJ = OFF_GB + D_MODEL

LANES = 128
BF16_ROWS = 16
F32_ROWS = 8
VMEM_LIMIT_BYTES = 60000 * 1024

IN_TM = 512
IN_TN = 2176
MIX_TQ = 256
FFN_TM = 512
FFN_TF = 512
ROW_CHUNK = 128
COL_CHUNK = 512

BF16 = jnp.bfloat16
F32 = jnp.float32


def _rmsnorm(x, g):
    ms = jnp.mean(x * x, axis=-1, keepdims=True)
    return x * lax.rsqrt(ms + EPS) * g


def _in_proj_kernel(x_ref, g_ref, w_ref, z_ref, u_ref):
    @pl.when(pl.program_id(1) == 0)
    def _():
        def body(r, carry):
            rows = pl.ds(pl.multiple_of(r * ROW_CHUNK, ROW_CHUNK), ROW_CHUNK)
            u_ref[rows, :] = _rmsnorm(x_ref[rows, :], g_ref[...]).astype(BF16)
            return carry
        lax.fori_loop(0, x_ref.shape[0] // ROW_CHUNK, body, 0)

    z_ref[...] = jnp.dot(u_ref[...], w_ref[...],
                         preferred_element_type=F32).astype(z_ref.dtype)


def _in_proj(x, g, w):
    s, d = x.shape
    n = w.shape[1]
    return pl.pallas_call(
        _in_proj_kernel,
        grid=(s // IN_TM, n // IN_TN),
        in_specs=[
            pl.BlockSpec((IN_TM, d), lambda i, j: (i, 0)),
            pl.BlockSpec((1, d), lambda i, j: (0, 0)),
            pl.BlockSpec((d, IN_TN), lambda i, j: (0, j)),
        ],
        out_specs=pl.BlockSpec((IN_TM, IN_TN), lambda i, j: (i, j)),
        out_shape=jax.ShapeDtypeStruct((s, n), BF16),
        scratch_shapes=[pltpu.VMEM((IN_TM, d), BF16)],
        compiler_params=pltpu.CompilerParams(
            dimension_semantics=("arbitrary", "arbitrary"),
            vmem_limit_bytes=VMEM_LIMIT_BYTES),
        name="in_proj",
    )(x, g, w)


def _rope(x, cos, sin_signed, first_half):
    partner = jnp.where(first_half, pltpu.roll(x, LANES - HEAD_DIM // 2, 1),
                        pltpu.roll(x, HEAD_DIM // 2, 1))
    return x * cos + partner * sin_signed


def _mixer_kernel(sink_ref, x_ref, z_ref, kvp_ref, kvn_ref, cp_ref, vp_ref, cn_ref, vn_ref,
                  cos_ref, sin_ref, cosp_ref, sinp_ref, cosn_ref, sinn_ref,
                  bg_ref, cw_ref, woa_ref, wo_ref, wmix_ref,
                  h1_ref,
                  qs_ref, kk_ref, vv_ref, cvs_ref, att_ref, ya_ref, mix_ref, *, seq_len):
    i = pl.program_id(0)
    nt = pl.num_programs(0)
    tq = x_ref.shape[0]
    lane = lax.broadcasted_iota(jnp.int32, (1, LANES), 1)
    first_half = (lane % HEAD_DIM) < (HEAD_DIM // 2)
    low_head = lane < HEAD_DIM
    scale = HEAD_DIM ** -0.5

    cos = cos_ref[...]
    sin = sin_ref[...]
    for g in range(D_ATTN // LANES):
        xq = z_ref[:, OFF_Q + g * LANES:OFF_Q + (g + 1) * LANES].astype(F32)
        qs_ref[:, g * LANES:(g + 1) * LANES] = (_rope(xq, cos, sin, first_half) * scale).astype(BF16)

    def put_kv(kv, c, s, row0):
        n = kv.shape[0]
        for pr in range(N_KV_HEADS // 2):
            kf = _rope(kv[:, pr * LANES:(pr + 1) * LANES].astype(F32), c, s, first_half)
            kr = pltpu.roll(kf, HEAD_DIM, 1)
            kk_ref[2 * pr, row0:row0 + n, :] = jnp.where(low_head, kf, kr).astype(BF16)
            kk_ref[2 * pr + 1, row0:row0 + n, :] = jnp.where(low_head, kr, kf).astype(BF16)
            vf = kv[:, D_KV + pr * LANES:D_KV + (pr + 1) * LANES].astype(F32)
            vr = pltpu.roll(vf, HEAD_DIM, 1)
            vv_ref[2 * pr, row0:row0 + n, :] = jnp.where(low_head, vf, vr).astype(BF16)
            vv_ref[2 * pr + 1, row0:row0 + n, :] = jnp.where(low_head, vr, vf).astype(BF16)

    put_kv(kvp_ref[...], cosp_ref[...], sinp_ref[...], 0)
    put_kv(z_ref[:, OFF_K:OFF_K + 2 * D_KV], cos, sin, BLOCK)
    put_kv(kvn_ref[...], cosn_ref[...], sinn_ref[...], BLOCK + tq)

    r_io = lax.broadcasted_iota(jnp.int32, (BLOCK, 3 * BLOCK), 0)
    c_io = lax.broadcasted_iota(jnp.int32, (BLOCK, 3 * BLOCK), 1)
    band = (c_io >= r_io) & (c_io <= r_io + 2 * WINDOW)
    for b in range(tq // BLOCK):
        kpos = c_io + (i * tq + (b - 1) * BLOCK)
        mask = band & (kpos >= 0) & (kpos < seq_len)
        rows = slice(b * BLOCK, (b + 1) * BLOCK)
        for h in range(N_KV_HEADS):
            kk = kk_ref[h, b * BLOCK:(b + 3) * BLOCK, :]
            vv = vv_ref[h, b * BLOCK:(b + 3) * BLOCK, :]
            for pr in range(2):
                grp = 2 * h + pr
                qpair = qs_ref[rows, grp * LANES:(grp + 1) * LANES]
                outs = []
                for half in range(2):
                    head = 2 * grp + half
                    sel = low_head if half == 0 else jnp.logical_not(low_head)
                    qm = jnp.where(sel, qpair, jnp.zeros_like(qpair))
                    sc = lax.dot_general(qm, kk, (((1,), (1,)), ((), ())),
                                         preferred_element_type=F32)
                    sc = jnp.where(mask, sc, -jnp.inf)
                    sink = sink_ref[head]
                    m = jnp.maximum(jnp.max(sc, axis=-1, keepdims=True), sink)
                    p = jnp.exp(sc - m)
                    denom = jnp.sum(p, axis=-1, keepdims=True) + jnp.exp(sink - m)
                    o = jnp.dot(p.astype(BF16), vv, preferred_element_type=F32)
                    outs.append(o * (1.0 / denom))
                att_ref[rows, grp * LANES:(grp + 1) * LANES] = (
                    jnp.where(low_head, outs[0], outs[1]).astype(BF16))

    zero8 = jnp.zeros((F32_ROWS, LANES), F32)
    for g in range(D_CONV // LANES):
        cols = slice(g * LANES, (g + 1) * LANES)
        zc = slice(OFF_C + g * LANES, OFF_C + (g + 1) * LANES)
        zv = slice(OFF_VA + g * LANES, OFF_VA + (g + 1) * LANES)
        zb = slice(OFF_B + g * LANES, OFF_B + (g + 1) * LANES)
        prev = (cp_ref[:, cols].astype(F32) * vp_ref[:, cols].astype(F32))[F32_ROWS:, :]
        nxt = (cn_ref[:, cols].astype(F32) * vn_ref[:, cols].astype(F32))[:F32_ROWS, :]
        cvs_ref[0:F32_ROWS, cols] = jnp.where(i > 0, prev, zero8)
        cvs_ref[F32_ROWS:F32_ROWS + tq, cols] = z_ref[:, zc].astype(F32) * z_ref[:, zv].astype(F32)
        cvs_ref[F32_ROWS + tq:2 * F32_ROWS + tq, cols] = jnp.where(i < nt - 1, nxt, zero8)
        conv = (cvs_ref[F32_ROWS - 1:F32_ROWS - 1 + tq, cols] * cw_ref[0:1, cols]
                + cvs_ref[F32_ROWS:F32_ROWS + tq, cols] * cw_ref[1:2, cols]
                + cvs_ref[F32_ROWS + 1:F32_ROWS + 1 + tq, cols] * cw_ref[2:3, cols])
        ya_ref[:, cols] = (z_ref[:, zb].astype(F32) * conv).astype(BF16)

    for c in range(D_MODEL // COL_CHUNK):
        cols = slice(c * COL_CHUNK, (c + 1) * COL_CHUNK)
        y_a = jnp.dot(ya_ref[...], woa_ref[:, cols], preferred_element_type=F32)
        y_b = jnp.dot(att_ref[...], wo_ref[:, cols], preferred_element_type=F32)
        g_a = jax.nn.sigmoid(
            z_ref[:, OFF_GA + c * COL_CHUNK:OFF_GA + (c + 1) * COL_CHUNK].astype(F32)
            + bg_ref[:, cols])
        g_b = jax.nn.sigmoid(
            z_ref[:, OFF_GB + c * COL_CHUNK:OFF_GB + (c + 1) * COL_CHUNK].astype(F32)
            + bg_ref[:, D_MODEL + c * COL_CHUNK:D_MODEL + (c + 1) * COL_CHUNK])
        mix_ref[:, cols] = (g_a * y_a + g_b * y_b).astype(BF16)

    for c in range(D_MODEL // COL_CHUNK):
        cols = slice(c * COL_CHUNK, (c + 1) * COL_CHUNK)
        h1_ref[:, cols] = x_ref[:, cols] + jnp.dot(mix_ref[...], wmix_ref[:, cols],
                                                   preferred_element_type=F32)


def _mixer(x, z, cos_t, sin_t, sink, b_gate, conv_w, w_out_a, w_o, w_mix):
    s, d = x.shape
    tq = MIX_TQ
    nt = s // tq
    kvb = tq // BLOCK
    cb = tq // BF16_ROWS
    kv_col = OFF_K // (2 * D_KV)
    assert OFF_K % (2 * D_KV) == 0 and OFF_C % D_CONV == 0 and OFF_VA % D_CONV == 0

    def prev_blk(i, per):
        return jnp.maximum(i * per - 1, 0)

    def next_blk(i, per, total):
        return jnp.minimum((i + 1) * per, total - 1)

    const = lambda i: (0, 0)
    single = dict(pipeline_mode=pl.Buffered(1))
    in_specs = [
        pl.BlockSpec(memory_space=pltpu.SMEM),
        pl.BlockSpec((tq, d), lambda i: (i, 0)),
        pl.BlockSpec((tq, D_IN_PROJ), lambda i: (i, 0)),
        pl.BlockSpec((BLOCK, 2 * D_KV), lambda i: (prev_blk(i, kvb), kv_col)),
        pl.BlockSpec((BLOCK, 2 * D_KV), lambda i: (next_blk(i, kvb, s // BLOCK), kv_col)),
        pl.BlockSpec((BF16_ROWS, D_CONV), lambda i: (prev_blk(i, cb), OFF_C // D_CONV)),
        pl.BlockSpec((BF16_ROWS, D_CONV), lambda i: (prev_blk(i, cb), OFF_VA // D_CONV)),
        pl.BlockSpec((BF16_ROWS, D_CONV), lambda i: (next_blk(i, cb, s // BF16_ROWS), OFF_C // D_CONV)),
        pl.BlockSpec((BF16_ROWS, D_CONV), lambda i: (next_blk(i, cb, s // BF16_ROWS), OFF_VA // D_CONV)),
        pl.BlockSpec((tq, LANES), lambda i: (i, 0)),
        pl.BlockSpec((tq, LANES), lambda i: (i, 0)),
        pl.BlockSpec((BLOCK, LANES), lambda i: (prev_blk(i, kvb), 0)),
        pl.BlockSpec((BLOCK, LANES), lambda i: (prev_blk(i, kvb), 0)),
        pl.BlockSpec((BLOCK, LANES), lambda i: (next_blk(i, kvb, s // BLOCK), 0)),
        pl.BlockSpec((BLOCK, LANES), lambda i: (next_blk(i, kvb, s // BLOCK), 0)),
        pl.BlockSpec((1, 2 * d), const),
        pl.BlockSpec((3, D_CONV), const),
        pl.BlockSpec((D_CONV, d), const, **single),
        pl.BlockSpec((D_ATTN, d), const, **single),
        pl.BlockSpec((d, d), const, **single),
    ]
    r_ext = tq + 2 * BLOCK
    return pl.pallas_call(
        functools.partial(_mixer_kernel, seq_len=s),
        grid=(nt,),
        in_specs=in_specs,
        out_specs=pl.BlockSpec((tq, d), lambda i: (i, 0)),
        out_shape=jax.ShapeDtypeStruct((s, d), F32),
        scratch_shapes=[
            pltpu.VMEM((tq, D_ATTN), BF16),
            pltpu.VMEM((N_KV_HEADS, r_ext, LANES), BF16),
            pltpu.VMEM((N_KV_HEADS, r_ext, LANES), BF16),
            pltpu.VMEM((tq + 2 * F32_ROWS, D_CONV), F32),
            pltpu.VMEM((tq, D_ATTN), BF16),
            pltpu.VMEM((tq, D_CONV), BF16),
            pltpu.VMEM((tq, d), BF16),
        ],
        compiler_params=pltpu.CompilerParams(
            dimension_semantics=("arbitrary",),
            vmem_limit_bytes=VMEM_LIMIT_BYTES),
        name="mixer",
    )(sink, x, z, z, z, z, z, z, z, cos_t, sin_t, cos_t, sin_t, cos_t, sin_t,
      b_gate, conv_w, w_out_a, w_o, w_mix)


def _ffn_kernel(h_ref, hp_ref, hn_ref, g2_ref, wa_ref, wg_ref, cwa_ref, cwg_ref,
                ba_ref, bgt_ref, wd_ref, gf_ref,
                o_ref,
                u_ref, sa_ref, sg_ref, act_ref, *, final_norm):
    i = pl.program_id(0)
    j = pl.program_id(1)
    nt = pl.num_programs(0)
    nf = pl.num_programs(1)
    tm = h_ref.shape[0]
    halo = BF16_ROWS

    @pl.when(j == 0)
    def _():
        g2 = g2_ref[...]
        zero = jnp.zeros((halo, h_ref.shape[1]), F32)
        u_ref[0:halo, :] = jnp.where(i > 0, _rmsnorm(hp_ref[...], g2), zero).astype(BF16)
        u_ref[halo + tm:2 * halo + tm, :] = jnp.where(
            i < nt - 1, _rmsnorm(hn_ref[...], g2), zero).astype(BF16)

        def body(r, carry):
            r0 = pl.multiple_of(r * ROW_CHUNK, ROW_CHUNK)
            u_ref[pl.ds(halo + r0, ROW_CHUNK), :] = _rmsnorm(
                h_ref[pl.ds(r0, ROW_CHUNK), :], g2).astype(BF16)
            o_ref[pl.ds(r0, ROW_CHUNK), :] = jnp.zeros((ROW_CHUNK, o_ref.shape[1]), F32)
            return carry
        lax.fori_loop(0, tm // ROW_CHUNK, body, 0)

    sa_ref[...] = jnp.dot(u_ref[...], wa_ref[...], preferred_element_type=F32)
    sg_ref[...] = jnp.dot(u_ref[...], wg_ref[...], preferred_element_type=F32)

    def conv(s_ref, cw_ref, b_ref, r0):
        return (s_ref[r0 + halo - 1:r0 + halo - 1 + ROW_CHUNK, :] * cw_ref[0:1, :]
                + s_ref[r0 + halo:r0 + halo + ROW_CHUNK, :] * cw_ref[1:2, :]
                + s_ref[r0 + halo + 1:r0 + halo + 1 + ROW_CHUNK, :] * cw_ref[2:3, :]) + b_ref[...]

    for r0 in range(0, tm, ROW_CHUNK):
        a = conv(sa_ref, cwa_ref, ba_ref, r0)
        g = conv(sg_ref, cwg_ref, bgt_ref, r0)
        act_ref[r0:r0 + ROW_CHUNK, :] = (a * jax.nn.sigmoid(a) * g).astype(BF16)

    for c in range(o_ref.shape[1] // COL_CHUNK):
        cols = slice(c * COL_CHUNK, (c + 1) * COL_CHUNK)
        o_ref[:, cols] += jnp.dot(act_ref[...], wd_ref[:, cols], preferred_element_type=F32)

    @pl.when(j == nf - 1)
    def _():
        def body(r, carry):
            rows = pl.ds(pl.multiple_of(r * ROW_CHUNK, ROW_CHUNK), ROW_CHUNK)
            h = h_ref[rows, :] + o_ref[rows, :]
            if final_norm:
                h = _rmsnorm(h, gf_ref[...])
            o_ref[rows, :] = h
            return carry
        lax.fori_loop(0, tm // ROW_CHUNK, body, 0)


def _ffn(h, g2, w_up, conv_w, conv_b, w_down, gf, final_norm):
    s, d = h.shape
    tm, tf = FFN_TM, FFN_TF
    nt, nf = s // tm, D_FF // tf
    hb = tm // BF16_ROWS
    in_specs = [
        pl.BlockSpec((tm, d), lambda i, j: (i, 0)),
        pl.BlockSpec((BF16_ROWS, d), lambda i, j: (jnp.maximum(i * hb - 1, 0), 0)),
        pl.BlockSpec((BF16_ROWS, d), lambda i, j: (jnp.minimum((i + 1) * hb, s // BF16_ROWS - 1), 0)),
        pl.BlockSpec((1, d), lambda i, j: (0, 0)),
        pl.BlockSpec((d, tf), lambda i, j: (0, j)),
        pl.BlockSpec((d, tf), lambda i, j: (0, nf + j)),
        pl.BlockSpec((3, tf), lambda i, j: (0, j)),
        pl.BlockSpec((3, tf), lambda i, j: (0, nf + j)),
        pl.BlockSpec((1, tf), lambda i, j: (0, j)),
        pl.BlockSpec((1, tf), lambda i, j: (0, nf + j)),
        pl.BlockSpec((tf, d), lambda i, j: (j, 0)),
        pl.BlockSpec((1, d), lambda i, j: (0, 0)),
    ]
    return pl.pallas_call(
        functools.partial(_ffn_kernel, final_norm=final_norm),
        grid=(nt, nf),
        in_specs=in_specs,
        out_specs=pl.BlockSpec((tm, d), lambda i, j: (i, 0)),
        out_shape=jax.ShapeDtypeStruct((s, d), F32),
        scratch_shapes=[
            pltpu.VMEM((tm + 2 * BF16_ROWS, d), BF16),
            pltpu.VMEM((tm + 2 * BF16_ROWS, tf), F32),
            pltpu.VMEM((tm + 2 * BF16_ROWS, tf), F32),
            pltpu.VMEM((tm, tf), BF16),
        ],
        compiler_params=pltpu.CompilerParams(
            dimension_semantics=("arbitrary", "arbitrary"),
            vmem_limit_bytes=VMEM_LIMIT_BYTES),
        name="ffn",
    )(h, h, h, g2, w_up, w_up, conv_w, conv_w, conv_b, conv_b, w_down, gf)


def _rope_tables(seq_len):
    half = HEAD_DIM // 2
    inv_freq = ROPE_THETA ** (-jnp.arange(0, half, dtype=F32) / half)
    ang = jnp.arange(seq_len, dtype=F32)[:, None] * inv_freq[None, :]
    cos = jnp.cos(ang)
    sin = jnp.sin(ang)
    reps = LANES // HEAD_DIM
    cos_t = jnp.tile(cos, (1, 2 * reps))
    sin_t = jnp.tile(jnp.concatenate([-sin, sin], axis=1), (1, reps))
    return cos_t, sin_t


def kernel(x, norm_mix_g, w_in, b_gate, conv_a_w, w_out_a, sink_logits, w_o_attn, w_mix_out,
           norm_ffn_g, ffn_w_up, ffn_conv_w, ffn_conv_b, ffn_w_down, norm_final_g):
    b, s, d = x.shape
    depth = w_in.shape[0]
    cos_t, sin_t = _rope_tables(s)
    outs = []
    for bi in range(b):
        h = x[bi]
        for l in range(depth):
            z = _in_proj(h, norm_mix_g[l][None, :], w_in[l].astype(BF16))
            h = _mixer(h, z, cos_t, sin_t, sink_logits[l], b_gate[l][None, :], conv_a_w[l],
                       w_out_a[l].astype(BF16), w_o_attn[l].astype(BF16),
                       w_mix_out[l].astype(BF16))
            h = _ffn(h, norm_ffn_g[l][None, :], ffn_w_up[l].astype(BF16), ffn_conv_w[l],
                     ffn_conv_b[l][None, :], ffn_w_down[l].astype(BF16),
                     norm_final_g[None, :], final_norm=(l == depth - 1))
        outs.append(h[None])
    return outs[0] if b == 1 else jnp.concatenate(outs, axis=0)
```

```python
import functools

import jax
import jax.numpy as jnp
from jax import lax
from jax.experimental import pallas as pl
from jax.experimental.pallas import tpu as pltpu

D_MODEL = 2048
D_CONV = D_MODEL // 2
N_HEADS = 16
N_KV_HEADS = 4
HEAD_DIM = 64
D_ATTN = N_HEADS * HEAD_DIM
D_KV = N_KV_HEADS * HEAD_DIM
WINDOW = 128
BLOCK = 128
ROPE_THETA = 10000.0
D_FF = 5632
EPS = 1e-6

OFF_Q = 0
OFF_B = OFF_Q + D_ATTN
OFF_C = OFF_B + D_CONV
OFF_VA = OFF_C + D_CONV
OFF_GA = OFF_VA + D_CONV
OFF_GB = OFF_GA + D_MODEL
OFF_K = OFF_GB + D_MODEL
OFF_V = OFF_K + D_KV
D_IN_PROJ = OFF_V + D_KV
_SRC_B, _SRC_C, _SRC_VA = 0, D_CONV, 2 * D_CONV
_SRC_Q = 3 * D_CONV
_SRC_K = _SRC_Q + D_ATTN
_SRC_V = _SRC_K + D_KV
_SRC_GA = _SRC_V + D_KV
_SRC_END = _SRC_GA + 2 * D_MODEL

LANES = 128
BF16_ROWS = 16
F32_ROWS = 8
VMEM_LIMIT_BYTES = 60000 * 1024

IN_TM = 1024
IN_TN = 2176
MIX_TQ = 512
MERGE_TM = 512
FFN_TM = 1024
FFN_TF = 512
FFN_SUB = 256
ROW_CHUNK = 128
COL_CHUNK = 512

BF16 = jnp.bfloat16
F32 = jnp.float32


def _rmsnorm(x, g):
    ms = jnp.mean(x * x, axis=-1, keepdims=True)
    return x * lax.rsqrt(ms + EPS) * g


def _in_proj_kernel(x_ref, g_ref, w_ref, z_ref, u_ref):
    @pl.when(pl.program_id(1) == 0)
    def _():
        def body(r, carry):
            rows = pl.ds(pl.multiple_of(r * ROW_CHUNK, ROW_CHUNK), ROW_CHUNK)
            u_ref[rows, :] = _rmsnorm(x_ref[rows, :], g_ref[...]).astype(BF16)
            return carry
        lax.fori_loop(0, x_ref.shape[0] // ROW_CHUNK, body, 0)

    z_ref[...] = jnp.dot(u_ref[...], w_ref[...],
                         preferred_element_type=F32).astype(z_ref.dtype)


def _in_proj(x, g, w):
    s, d = x.shape
    n = w.shape[1]
    return pl.pallas_call(
        _in_proj_kernel,
        grid=(s // IN_TM, n // IN_TN),
        in_specs=[
            pl.BlockSpec((IN_TM, d), lambda i, j: (i, 0)),
            pl.BlockSpec((1, d), lambda i, j: (0, 0)),
            pl.BlockSpec((d, IN_TN), lambda i, j: (0, j)),
        ],
        out_specs=pl.BlockSpec((IN_TM, IN_TN), lambda i, j: (i, j)),
        out_shape=jax.ShapeDtypeStruct((s, n), BF16),
        scratch_shapes=[pltpu.VMEM((IN_TM, d), BF16)],
        compiler_params=pltpu.CompilerParams(
            dimension_semantics=("arbitrary", "arbitrary"),
            vmem_limit_bytes=VMEM_LIMIT_BYTES),
        name="in_proj",
    )(x, g, w)


def _rope(x, cos, sin_signed, first_half):
    partner = jnp.where(first_half, pltpu.roll(x, LANES - HEAD_DIM // 2, 1),
                        pltpu.roll(x, HEAD_DIM // 2, 1))
    return x * cos + partner * sin_signed


def _attn_conv_kernel(sink_ref, z_ref, kv_ref, kvp_ref, kvn_ref, cp_ref, vp_ref, cn_ref, vn_ref,
                      cos_ref, sin_ref, cosp_ref, sinp_ref, cosn_ref, sinn_ref, cw_ref,
                      att_ref, ya_ref,
                      qst_ref, kp_ref, vt_ref, cvs_ref, *, seq_len):
    i = pl.program_id(0)
    nt = pl.num_programs(0)
    tq = z_ref.shape[0]
    nb = tq // BLOCK
    grp = N_HEADS // N_KV_HEADS
    lane = lax.broadcasted_iota(jnp.int32, (1, LANES), 1)
    first_half = (lane % HEAD_DIM) < (HEAD_DIM // 2)
    low_head = lane < HEAD_DIM
    scale = HEAD_DIM ** -0.5

    cos = cos_ref[...]
    sin = sin_ref[...]
    for g in range(D_ATTN // LANES):
        h = g // 2
        xq = z_ref[:, OFF_Q + g * LANES:OFF_Q + (g + 1) * LANES].astype(F32)
        xr = _rope(xq, cos, sin, first_half) * scale
        rolled = pltpu.roll(xr, HEAD_DIM, 1)
        zero = jnp.zeros_like(xr)
        if h % 2 == 0:
            even, odd = jnp.where(low_head, xr, zero), jnp.where(low_head, rolled, zero)
        else:
            even, odd = jnp.where(low_head, zero, rolled), jnp.where(low_head, zero, xr)
        j0 = 2 * (g % 2)
        for b in range(nb):
            rows = slice(b * BLOCK, (b + 1) * BLOCK)
            qst_ref[h, b, j0 * BLOCK:(j0 + 1) * BLOCK, :] = even[rows].astype(BF16)
            qst_ref[h, b, (j0 + 1) * BLOCK:(j0 + 2) * BLOCK, :] = odd[rows].astype(BF16)

    def put_kv(kv, c, s, row0):
        n = kv.shape[0]
        for pr in range(N_KV_HEADS // 2):
            kf = _rope(kv[:, pr * LANES:(pr + 1) * LANES].astype(F32), c, s, first_half)
            kp_ref[pr, row0:row0 + n, :] = kf.astype(BF16)
            for blk in range(n // BLOCK):
                vb = kv[blk * BLOCK:(blk + 1) * BLOCK,
                        D_KV + pr * LANES:D_KV + (pr + 1) * LANES].astype(F32)
                vt_ref[pr, :, row0 + blk * BLOCK:row0 + (blk + 1) * BLOCK] = vb.T.astype(BF16)

    put_kv(kvp_ref[...], cosp_ref[...], sinp_ref[...], 0)
    put_kv(kv_ref[...], cos, sin, BLOCK)
    put_kv(kvn_ref[...], cosn_ref[...], sinn_ref[...], BLOCK + tq)

    c_io = lax.broadcasted_iota(jnp.int32, (3 * BLOCK, BLOCK), 0)
    r_io = lax.broadcasted_iota(jnp.int32, (3 * BLOCK, BLOCK), 1)
    band = (c_io >= r_io) & (c_io <= r_io + 2 * WINDOW)

    def scores(b, h):
        kwin = kp_ref[h // 2, b * BLOCK:(b + 3) * BLOCK, :]
        return lax.dot_general(kwin, qst_ref[h, b], (((1,), (1,)), ((), ())),
                               preferred_element_type=F32)

    def finish(b, h, s_t):
        kpos = c_io + (i * tq + (b - 1) * BLOCK)
        mask = band & (kpos >= 0) & (kpos < seq_len)
        probs, inv = [], []
        for j in range(grp):
            sink = sink_ref[grp * h + j]
            sc = jnp.where(mask, s_t[:, j * BLOCK:(j + 1) * BLOCK], -jnp.inf)
            m = jnp.maximum(jnp.max(sc, axis=0, keepdims=True), sink)
            p = jnp.exp(sc - m)
            denom = jnp.sum(p, axis=0, keepdims=True) + jnp.exp(sink - m)
            probs.append(p.astype(BF16))
            inv.append(1.0 / denom)
        p_t = jnp.concatenate(probs, axis=1)
        o_t = jnp.dot(vt_ref[h // 2, :, b * BLOCK:(b + 3) * BLOCK], p_t,
                      preferred_element_type=F32)
        r0 = HEAD_DIM * (h % 2)
        for pr in range(grp // 2):
            top = o_t[r0:r0 + HEAD_DIM, (2 * pr) * BLOCK:(2 * pr + 1) * BLOCK] * inv[2 * pr]
            bot = o_t[r0:r0 + HEAD_DIM, (2 * pr + 1) * BLOCK:(2 * pr + 2) * BLOCK] * inv[2 * pr + 1]
            pair_t = jnp.concatenate([top, bot], axis=0)
            g = (grp // 2) * h + pr
            att_ref[b * BLOCK:(b + 1) * BLOCK, g * LANES:(g + 1) * LANES] = pair_t.T.astype(BF16)

    pending = None
    for b in range(nb):
        for h in range(N_KV_HEADS):
            s_t = scores(b, h)
            if pending is not None:
                finish(*pending)
            pending = (b, h, s_t)
    finish(*pending)

    zero8 = jnp.zeros((F32_ROWS, LANES), F32)
    for g in range(D_CONV // LANES):
        cols = slice(g * LANES, (g + 1) * LANES)
        zc = slice(OFF_C + g * LANES, OFF_C + (g + 1) * LANES)
        zv = slice(OFF_VA + g * LANES, OFF_VA + (g + 1) * LANES)
        zb = slice(OFF_B + g * LANES, OFF_B + (g + 1) * LANES)
        prev = (cp_ref[:, cols].astype(F32) * vp_ref[:, cols].astype(F32))[F32_ROWS:, :]
        nxt = (cn_ref[:, cols].astype(F32) * vn_ref[:, cols].astype(F32))[:F32_ROWS, :]
        cvs_ref[0:F32_ROWS, cols] = jnp.where(i > 0, prev, zero8)
        cvs_ref[F32_ROWS:F32_ROWS + tq, cols] = z_ref[:, zc].astype(F32) * z_ref[:, zv].astype(F32)
        cvs_ref[F32_ROWS + tq:2 * F32_ROWS + tq, cols] = jnp.where(i < nt - 1, nxt, zero8)
        conv = (cvs_ref[F32_ROWS - 1:F32_ROWS - 1 + tq, cols] * cw_ref[0:1, cols]
                + cvs_ref[F32_ROWS:F32_ROWS + tq, cols] * cw_ref[1:2, cols]
                + cvs_ref[F32_ROWS + 1:F32_ROWS + 1 + tq, cols] * cw_ref[2:3, cols])
        ya_ref[:, cols] = (z_ref[:, zb].astype(F32) * conv).astype(BF16)


def _attn_conv(z, cos_t, sin_t, sink, conv_w):
    s = z.shape[0]
    tq = MIX_TQ
    nt = s // tq
    kvb = tq // BLOCK
    cb = tq // BF16_ROWS
    qbcv = 4 * D_CONV
    kv_w = 2 * D_KV
    assert OFF_Q == 0 and OFF_VA + D_CONV == qbcv and OFF_K % kv_w == 0 and OFF_V == OFF_K + D_KV

    def prev_blk(i, per):
        return jnp.maximum(i * per - 1, 0)

    def next_blk(i, per, total):
        return jnp.minimum((i + 1) * per, total - 1)

    in_specs = [
        pl.BlockSpec(memory_space=pltpu.SMEM),
        pl.BlockSpec((tq, qbcv), lambda i: (i, 0)),
        pl.BlockSpec((tq, kv_w), lambda i: (i, OFF_K // kv_w)),
        pl.BlockSpec((BLOCK, kv_w), lambda i: (prev_blk(i, kvb), OFF_K // kv_w)),
        pl.BlockSpec((BLOCK, kv_w), lambda i: (next_blk(i, kvb, s // BLOCK), OFF_K // kv_w)),
        pl.BlockSpec((BF16_ROWS, D_CONV), lambda i: (prev_blk(i, cb), OFF_C // D_CONV)),
        pl.BlockSpec((BF16_ROWS, D_CONV), lambda i: (prev_blk(i, cb), OFF_VA // D_CONV)),
        pl.BlockSpec((BF16_ROWS, D_CONV), lambda i: (next_blk(i, cb, s // BF16_ROWS), OFF_C // D_CONV)),
        pl.BlockSpec((BF16_ROWS, D_CONV), lambda i: (next_blk(i, cb, s // BF16_ROWS), OFF_VA // D_CONV)),
        pl.BlockSpec((tq, LANES), lambda i: (i, 0)),
        pl.BlockSpec((tq, LANES), lambda i: (i, 0)),
        pl.BlockSpec((BLOCK, LANES), lambda i: (prev_blk(i, kvb), 0)),
        pl.BlockSpec((BLOCK, LANES), lambda i: (prev_blk(i, kvb), 0)),
        pl.BlockSpec((BLOCK, LANES), lambda i: (next_blk(i, kvb, s // BLOCK), 0)),
        pl.BlockSpec((BLOCK, LANES), lambda i: (next_blk(i, kvb, s // BLOCK), 0)),
        pl.BlockSpec((3, D_CONV), lambda i: (0, 0)),
    ]
    r_ext = tq + 2 * BLOCK
    grp = N_HEADS // N_KV_HEADS
    return pl.pallas_call(
        functools.partial(_attn_conv_kernel, seq_len=s),
        grid=(nt,),
        in_specs=in_specs,
        out_specs=[pl.BlockSpec((tq, D_ATTN), lambda i: (i, 0)),
                   pl.BlockSpec((tq, D_CONV), lambda i: (i, 0))],
        out_shape=[jax.ShapeDtypeStruct((s, D_ATTN), BF16),
                   jax.ShapeDtypeStruct((s, D_CONV), BF16)],
        scratch_shapes=[
            pltpu.VMEM((N_KV_HEADS, tq // BLOCK, grp * BLOCK, LANES), BF16),
            pltpu.VMEM((N_KV_HEADS // 2, r_ext, LANES), BF16),
            pltpu.VMEM((N_KV_HEADS // 2, LANES, r_ext), BF16),
            pltpu.VMEM((tq + 2 * F32_ROWS, D_CONV), F32),
        ],
        compiler_params=pltpu.CompilerParams(
            dimension_semantics=("arbitrary",),
            vmem_limit_bytes=VMEM_LIMIT_BYTES),
        name="attn_conv",
    )(sink, z, z, z, z, z, z, z, z, cos_t, sin_t, cos_t, sin_t, cos_t, sin_t, conv_w)


def _merge_kernel(x_ref, ya_ref, att_ref, gl_ref, bg_ref, woa_ref, wo_ref, wmix_ref,
                  h1_ref, ys_a_ref, ys_b_ref, mix_ref):
    tm, d = x_ref.shape
    for c in range(d // COL_CHUNK):
        cols = slice(c * COL_CHUNK, (c + 1) * COL_CHUNK)
        gcols = slice(d + c * COL_CHUNK, d + (c + 1) * COL_CHUNK)
        ys_a_ref[c % 2] = jnp.dot(ya_ref[...], woa_ref[:, cols], preferred_element_type=F32)
        ys_b_ref[c % 2] = jnp.dot(att_ref[...], wo_ref[:, cols], preferred_element_type=F32)
        for r0 in range(0, tm, ROW_CHUNK):
            rows = slice(r0, r0 + ROW_CHUNK)
            g_a = jax.nn.sigmoid(gl_ref[rows, cols].astype(F32) + bg_ref[:, cols])
            g_b = jax.nn.sigmoid(gl_ref[rows, gcols].astype(F32) + bg_ref[:, gcols])
            mix_ref[rows, cols] = (g_a * ys_a_ref[c % 2, rows, :]
                                   + g_b * ys_b_ref[c % 2, rows, :]).astype(BF16)
    for c in range(d // COL_CHUNK):
        cols = slice(c * COL_CHUNK, (c + 1) * COL_CHUNK)
        h1_ref[:, cols] = x_ref[:, cols] + jnp.dot(mix_ref[...], wmix_ref[:, cols],
                                                   preferred_element_type=F32)


def _merge(x, ya, att, z, b_gate, w_out_a, w_o, w_mix):
    s, d = x.shape
    tm = MERGE_TM
    assert OFF_GA % (2 * d) == 0 and OFF_GB == OFF_GA + d
    const = lambda i: (0, 0)
    single = dict(pipeline_mode=pl.Buffered(1))
    in_specs = [
        pl.BlockSpec((tm, d), lambda i: (i, 0)),
        pl.BlockSpec((tm, D_CONV), lambda i: (i, 0)),
        pl.BlockSpec((tm, D_ATTN), lambda i: (i, 0)),
        pl.BlockSpec((tm, 2 * d), lambda i: (i, OFF_GA // (2 * d))),
        pl.BlockSpec((1, 2 * d), const),
        pl.BlockSpec((D_CONV, d), const, **single),
        pl.BlockSpec((D_ATTN, d), const, **single),
        pl.BlockSpec((d, d), const, **single),
    ]
    return pl.pallas_call(
        _merge_kernel,
        grid=(s // tm,),
        in_specs=in_specs,
        out_specs=pl.BlockSpec((tm, d), lambda i: (i, 0)),
        out_shape=jax.ShapeDtypeStruct((s, d), F32),
        scratch_shapes=[
            pltpu.VMEM((2, tm, COL_CHUNK), F32),
            pltpu.VMEM((2, tm, COL_CHUNK), F32),
            pltpu.VMEM((tm, d), BF16),
        ],
        compiler_params=pltpu.CompilerParams(
            dimension_semantics=("arbitrary",),
            vmem_limit_bytes=VMEM_LIMIT_BYTES),
        name="merge",
    )(x, ya, att, z, b_gate, w_out_a, w_o, w_mix)


def _ffn_kernel(h_ref, hp_ref, hn_ref, g2_ref, wa_ref, wg_ref, cwa_ref, cwg_ref,
                ba_ref, bgt_ref, wd_ref, gf_ref,
                o_ref,
                u_ref, sa_ref, sg_ref, act_ref, *, final_norm):
    i = pl.program_id(0)
    j = pl.program_id(1)
    nt = pl.num_programs(0)
    nf = pl.num_programs(1)
    tm = h_ref.shape[0]
    halo = BF16_ROWS

    @pl.when(j == 0)
    def _():
        g2 = g2_ref[...]
        zero = jnp.zeros((halo, h_ref.shape[1]), F32)
        u_ref[0:halo, :] = jnp.where(i > 0, _rmsnorm(hp_ref[...], g2), zero).astype(BF16)
        u_ref[halo + tm:2 * halo + tm, :] = jnp.where(
            i < nt - 1, _rmsnorm(hn_ref[...], g2), zero).astype(BF16)

        def body(r, carry):
            r0 = pl.multiple_of(r * ROW_CHUNK, ROW_CHUNK)
            u_ref[pl.ds(halo + r0, ROW_CHUNK), :] = _rmsnorm(
                h_ref[pl.ds(r0, ROW_CHUNK), :], g2).astype(BF16)
            o_ref[pl.ds(r0, ROW_CHUNK), :] = jnp.zeros((ROW_CHUNK, o_ref.shape[1]), F32)
            return carry
        lax.fori_loop(0, tm // ROW_CHUNK, body, 0)

    def conv(s_ref, k, cw_ref, b_ref, r0, cs):
        return (s_ref[k, r0 + halo - 1:r0 + halo - 1 + ROW_CHUNK, :] * cw_ref[0:1, cs]
                + s_ref[k, r0 + halo:r0 + halo + ROW_CHUNK, :] * cw_ref[1:2, cs]
                + s_ref[k, r0 + halo + 1:r0 + halo + 1 + ROW_CHUNK, :] * cw_ref[2:3, cs]) + b_ref[:, cs]

    n_sub = sa_ref.shape[0]
    for k in range(n_sub):
        cs = slice(k * FFN_SUB, (k + 1) * FFN_SUB)
        sa_ref[k] = jnp.dot(u_ref[...], wa_ref[:, cs], preferred_element_type=F32)
        sg_ref[k] = jnp.dot(u_ref[...], wg_ref[:, cs], preferred_element_type=F32)
    for k in range(n_sub):
        cs = slice(k * FFN_SUB, (k + 1) * FFN_SUB)
        for r0 in range(0, tm, ROW_CHUNK):
            a = conv(sa_ref, k, cwa_ref, ba_ref, r0, cs)
            g = conv(sg_ref, k, cwg_ref, bgt_ref, r0, cs)
            act_ref[k, r0:r0 + ROW_CHUNK, :] = (a * jax.nn.sigmoid(a) * g).astype(BF16)
        for c in range(o_ref.shape[1] // COL_CHUNK):
            cols = slice(c * COL_CHUNK, (c + 1) * COL_CHUNK)
            o_ref[:, cols] += jnp.dot(act_ref[k], wd_ref[cs, cols], preferred_element_type=F32)

    @pl.when(j == nf - 1)
    def _():
        def body(r, carry):
            rows = pl.ds(pl.multiple_of(r * ROW_CHUNK, ROW_CHUNK), ROW_CHUNK)
            h = h_ref[rows, :] + o_ref[rows, :]
            if final_norm:
                h = _rmsnorm(h, gf_ref[...])
            o_ref[rows, :] = h
            return carry
        lax.fori_loop(0, tm // ROW_CHUNK, body, 0)


def _ffn(h, g2, w_up, conv_w, conv_b, w_down, gf, final_norm):
    s, d = h.shape
    tm, tf = FFN_TM, FFN_TF
    nt, nf = s // tm, D_FF // tf
    hb = tm // BF16_ROWS
    in_specs = [
        pl.BlockSpec((tm, d), lambda i, j: (i, 0)),
        pl.BlockSpec((BF16_ROWS, d), lambda i, j: (jnp.maximum(i * hb - 1, 0), 0)),
        pl.BlockSpec((BF16_ROWS, d), lambda i, j: (jnp.minimum((i + 1) * hb, s // BF16_ROWS - 1), 0)),
        pl.BlockSpec((1, d), lambda i, j: (0, 0)),
        pl.BlockSpec((d, tf), lambda i, j: (0, j)),
        pl.BlockSpec((d, tf), lambda i, j: (0, nf + j)),
        pl.BlockSpec((3, tf), lambda i, j: (0, j)),
        pl.BlockSpec((3, tf), lambda i, j: (0, nf + j)),
        pl.BlockSpec((1, tf), lambda i, j: (0, j)),
        pl.BlockSpec((1, tf), lambda i, j: (0, nf + j)),
        pl.BlockSpec((tf, d), lambda i, j: (j, 0)),
        pl.BlockSpec((1, d), lambda i, j: (0, 0)),
    ]
    return pl.pallas_call(
        functools.partial(_ffn_kernel, final_norm=final_norm),
        grid=(nt, nf),
        in_specs=in_specs,
        out_specs=pl.BlockSpec((tm, d), lambda i, j: (i, 0)),
        out_shape=jax.ShapeDtypeStruct((s, d), F32),
        scratch_shapes=[
            pltpu.VMEM((tm + 2 * BF16_ROWS, d), BF16),
            pltpu.VMEM((tf // FFN_SUB, tm + 2 * BF16_ROWS, FFN_SUB), F32),
            pltpu.VMEM((tf // FFN_SUB, tm + 2 * BF16_ROWS, FFN_SUB), F32),
            pltpu.VMEM((tf // FFN_SUB, tm, FFN_SUB), BF16),
        ],
        compiler_params=pltpu.CompilerParams(
            dimension_semantics=("arbitrary", "arbitrary"),
            vmem_limit_bytes=VMEM_LIMIT_BYTES),
        name="ffn",
    )(h, h, h, g2, w_up, w_up, conv_w, conv_w, conv_b, conv_b, w_down, gf)


def _rope_tables(seq_len):
    half = HEAD_DIM // 2
    inv_freq = ROPE_THETA ** (-jnp.arange(0, half, dtype=F32) / half)
    ang = jnp.arange(seq_len, dtype=F32)[:, None] * inv_freq[None, :]
    cos = jnp.cos(ang)
    sin = jnp.sin(ang)
    reps = LANES // HEAD_DIM
    cos_t = jnp.tile(cos, (1, 2 * reps))
    sin_t = jnp.tile(jnp.concatenate([-sin, sin], axis=1), (1, reps))
    return cos_t, sin_t


def _regroup_in_proj(w):
    return jnp.concatenate([w[:, _SRC_Q:_SRC_K], w[:, _SRC_B:_SRC_Q],
                            w[:, _SRC_GA:_SRC_END], w[:, _SRC_K:_SRC_GA]], axis=1).astype(BF16)


def kernel(x, norm_mix_g, w_in, b_gate, conv_a_w, w_out_a, sink_logits, w_o_attn, w_mix_out,
           norm_ffn_g, ffn_w_up, ffn_conv_w, ffn_conv_b, ffn_w_down, norm_final_g):
    b, s, d = x.shape
    depth = w_in.shape[0]
    cos_t, sin_t = _rope_tables(s)
    outs = []
    for bi in range(b):
        h = x[bi]
        for l in range(depth):
            z = _in_proj(h, norm_mix_g[l][None, :], _regroup_in_proj(w_in[l]))
            att, ya = _attn_conv(z, cos_t, sin_t, sink_logits[l], conv_a_w[l])
            h = _merge(h, ya, att, z, b_gate[l][None, :], w_out_a[l].astype(BF16),
                       w_o_attn[l].astype(BF16), w_mix_out[l].astype(BF16))
            h = _ffn(h, norm_ffn_g[l][None, :], ffn_w_up[l].astype(BF16), ffn_conv_w[l],
                     ffn_conv_b[l][None, :], ffn_w_down[l].astype(BF16),
                     norm_final_g[None, :], final_norm=(l == depth - 1))
        outs.append(h[None])
    return outs[0] if b == 1 else jnp.concatenate(outs, axis=0)
```

```python
import functools

import jax
import jax.numpy as jnp
from jax import lax
from jax.experimental import pallas as pl
from jax.experimental.pallas import tpu as pltpu

D_MODEL = 2048
D_CONV = D_MODEL // 2
N_HEADS = 16
N_KV_HEADS = 4
HEAD_DIM = 64
D_ATTN = N_HEADS * HEAD_DIM
D_KV = N_KV_HEADS * HEAD_DIM
WINDOW = 128
BLOCK = 128
ROPE_THETA = 10000.0
D_FF = 5632
EPS = 1e-6

OFF_B = 0
OFF_C = OFF_B + D_CONV
OFF_VA = OFF_C + D_CONV
OFF_Q = OFF_VA + D_CONV
OFF_K = OFF_Q + D_ATTN
OFF_V = OFF_K + D_KV
OFF_GA = OFF_V + D_KV
OFF_GB = OFF_GA + D_MODEL
D_IN_PROJ = OFF_GB + D_MODEL
LOG2E = 1.4426950408889634

LANES = 128
BF16_ROWS = 16
F32_ROWS = 8
VMEM_LIMIT_BYTES = 60000 * 1024

IN_TM = 1024
IN_TN = 2176
MIX_TQ = 512
MERGE_TM = 512
FFN_TM = 1024
FFN_TF = 512
FFN_SUB = 256
FFN_DOWN_ROWS = 512
FFN_ACT_ROWS = 128
ROW_CHUNK = 128
COL_CHUNK = 512

BF16 = jnp.bfloat16
F32 = jnp.float32


def _rmsnorm(x, g):
    ms = jnp.mean(x * x, axis=-1, keepdims=True)
    return x * lax.rsqrt(ms + EPS) * g


def _in_proj_kernel(x_ref, g_ref, w_ref, z_ref, u_ref):
    @pl.when(pl.program_id(1) == 0)
    def _():
        def body(r, carry):
            rows = pl.ds(pl.multiple_of(r * ROW_CHUNK, ROW_CHUNK), ROW_CHUNK)
            u_ref[rows, :] = _rmsnorm(x_ref[rows, :], g_ref[...]).astype(BF16)
            return carry
        lax.fori_loop(0, x_ref.shape[0] // ROW_CHUNK, body, 0)

    z_ref[...] = jnp.dot(u_ref[...], w_ref[...],
                         preferred_element_type=F32).astype(z_ref.dtype)


def _in_proj(x, g, w):
    s, d = x.shape
    n = w.shape[1]
    return pl.pallas_call(
        _in_proj_kernel,
        grid=(s // IN_TM, n // IN_TN),
        in_specs=[
            pl.BlockSpec((IN_TM, d), lambda i, j: (i, 0)),
            pl.BlockSpec((1, d), lambda i, j: (0, 0)),
            pl.BlockSpec((d, IN_TN), lambda i, j: (0, j)),
        ],
        out_specs=pl.BlockSpec((IN_TM, IN_TN), lambda i, j: (i, j)),
        out_shape=jax.ShapeDtypeStruct((s, n), BF16),
        scratch_shapes=[pltpu.VMEM((IN_TM, d), BF16)],
        compiler_params=pltpu.CompilerParams(
            dimension_semantics=("arbitrary", "arbitrary"),
            vmem_limit_bytes=VMEM_LIMIT_BYTES),
        name="in_proj",
    )(x, g, w)


def _rope(x, cos, sin_signed, first_half):
    partner = jnp.where(first_half, pltpu.roll(x, LANES - HEAD_DIM // 2, 1),
                        pltpu.roll(x, HEAD_DIM // 2, 1))
    return x * cos + partner * sin_signed


def _attn_conv_kernel(sink_ref, z_ref, kv_ref, kvp_ref, kvn_ref, cp_ref, vp_ref, cn_ref, vn_ref,
                      cos_ref, sin_ref, cosp_ref, sinp_ref, cosn_ref, sinn_ref, cw_ref,
                      att_ref, ya_ref,
                      qst_ref, kz_ref, vt_ref, cvs_ref, *, seq_len):
    i = pl.program_id(0)
    nt = pl.num_programs(0)
    tq = z_ref.shape[0]
    nb = tq // BLOCK
    grp = N_HEADS // N_KV_HEADS
    lane = lax.broadcasted_iota(jnp.int32, (1, LANES), 1)
    first_half = (lane % HEAD_DIM) < (HEAD_DIM // 2)
    low_head = lane < HEAD_DIM
    scale = HEAD_DIM ** -0.5 * LOG2E

    for b in range(nb):
        rows = slice(b * BLOCK, (b + 1) * BLOCK)
        cos_b = cos_ref[rows, :]
        sin_b = sin_ref[rows, :]
        for g in range(D_ATTN // LANES):
            h = g // 2
            xq = z_ref[rows, OFF_Q + g * LANES:OFF_Q + (g + 1) * LANES].astype(F32)
            xr = _rope(xq, cos_b, sin_b, first_half) * scale
            rolled = pltpu.roll(xr, HEAD_DIM, 1)
            even, odd = (xr, rolled) if h % 2 == 0 else (rolled, xr)
            j0 = 2 * (g % 2)
            qst_ref[h, b, j0 * BLOCK:(j0 + 1) * BLOCK, :] = even.astype(BF16)
            qst_ref[h, b, (j0 + 1) * BLOCK:(j0 + 2) * BLOCK, :] = odd.astype(BF16)

    sub = lax.broadcasted_iota(jnp.int32, (HEAD_DIM, BLOCK), 0)
    ones_rows = jnp.where(sub == 0, 1.0, 0.0).astype(F32)

    def put_kv(kv_blk_ref, c_ref, s_ref, blk, row0):
        src = slice(blk * BLOCK, (blk + 1) * BLOCK)
        dst = slice(row0, row0 + BLOCK)
        c = c_ref[src, :]
        s = s_ref[src, :]
        for pr in range(N_KV_HEADS // 2):
            kf = _rope(kv_blk_ref[src, pr * LANES:(pr + 1) * LANES].astype(F32), c, s, first_half)
            zero = jnp.zeros_like(kf)
            kz_ref[2 * pr, dst, :] = jnp.where(low_head, kf, zero).astype(BF16)
            kz_ref[2 * pr + 1, dst, :] = jnp.where(low_head, zero, kf).astype(BF16)
            vb_t = kv_blk_ref[src, D_KV + pr * LANES:D_KV + (pr + 1) * LANES].astype(F32).T
            vt_ref[2 * pr, :, dst] = jnp.concatenate(
                [vb_t[:HEAD_DIM], ones_rows], axis=0).astype(BF16)
            vt_ref[2 * pr + 1, :, dst] = jnp.concatenate(
                [vb_t[HEAD_DIM:], ones_rows], axis=0).astype(BF16)

    put_kv(kvp_ref, cosp_ref, sinp_ref, 0, 0)
    for blk in range(nb):
        put_kv(kv_ref, cos_ref, sin_ref, blk, (blk + 1) * BLOCK)
    put_kv(kvn_ref, cosn_ref, sinn_ref, 0, BLOCK + tq)

    c_io = lax.broadcasted_iota(jnp.int32, (BLOCK, BLOCK), 0)
    r_io = lax.broadcasted_iota(jnp.int32, (BLOCK, BLOCK), 1)

    def scores(b, h):
        return lax.dot_general(kz_ref[h, b * BLOCK:(b + 3) * BLOCK, :], qst_ref[h, b],
                               (((1,), (1,)), ((), ())),
                               preferred_element_type=F32)

    def finish(b, h, s_t):
        base = i * tq + b * BLOCK
        mask_lo = (c_io >= r_io) & (c_io + (base - BLOCK) >= 0)
        mask_hi = (c_io <= r_io) & (c_io + (base + BLOCK) < seq_len)
        probs, sink_terms = [], []
        for j in range(grp):
            sink = sink_ref[grp * h + j] * LOG2E
            cols = slice(j * BLOCK, (j + 1) * BLOCK)
            lo = jnp.where(mask_lo, s_t[0:BLOCK, cols], -jnp.inf)
            mid = s_t[BLOCK:2 * BLOCK, cols]
            hi = jnp.where(mask_hi, s_t[2 * BLOCK:3 * BLOCK, cols], -jnp.inf)
            m = jnp.maximum(jnp.maximum(jnp.max(lo, axis=0, keepdims=True),
                                        jnp.max(hi, axis=0, keepdims=True)),
                            jnp.maximum(jnp.max(mid, axis=0, keepdims=True), sink))
            probs.append(jnp.concatenate(
                [jnp.exp2(lo - m), jnp.exp2(mid - m), jnp.exp2(hi - m)], axis=0).astype(BF16))
            sink_terms.append(jnp.exp2(sink - m))
        p_t = jnp.concatenate(probs, axis=1)
        o_t = jnp.dot(vt_ref[h, :, b * BLOCK:(b + 3) * BLOCK], p_t,
                      preferred_element_type=F32)
        outs = []
        for j in range(grp):
            cols = slice(j * BLOCK, (j + 1) * BLOCK)
            denom = o_t[HEAD_DIM:HEAD_DIM + 1, cols] + sink_terms[j]
            outs.append(o_t[:HEAD_DIM, cols] * (1.0 / denom))
        for pr in range(grp // 2):
            pair_t = jnp.concatenate([outs[2 * pr], outs[2 * pr + 1]], axis=0)
            g = (grp // 2) * h + pr
            att_ref[b * BLOCK:(b + 1) * BLOCK, g * LANES:(g + 1) * LANES] = pair_t.T.astype(BF16)

    pending = None
    for b in range(nb):
        for h in range(N_KV_HEADS):
            s_t = scores(b, h)
            if pending is not None:
                finish(*pending)
            pending = (b, h, s_t)
    finish(*pending)

    zero8 = jnp.zeros((F32_ROWS, LANES), F32)
    for g in range(D_CONV // LANES):
        cols = slice(g * LANES, (g + 1) * LANES)
        zc = slice(OFF_C + g * LANES, OFF_C + (g + 1) * LANES)
        zv = slice(OFF_VA + g * LANES, OFF_VA + (g + 1) * LANES)
        zb = slice(OFF_B + g * LANES, OFF_B + (g + 1) * LANES)
        prev = (cp_ref[:, cols].astype(F32) * vp_ref[:, cols].astype(F32))[F32_ROWS:, :]
        nxt = (cn_ref[:, cols].astype(F32) * vn_ref[:, cols].astype(F32))[:F32_ROWS, :]
        cvs_ref[0:F32_ROWS, cols] = jnp.where(i > 0, prev, zero8)
        for r0 in range(0, tq, ROW_CHUNK):
            rows = slice(r0, r0 + ROW_CHUNK)
            cvs_ref[F32_ROWS + r0:F32_ROWS + r0 + ROW_CHUNK, cols] = (
                z_ref[rows, zc].astype(F32) * z_ref[rows, zv].astype(F32))
        cvs_ref[F32_ROWS + tq:2 * F32_ROWS + tq, cols] = jnp.where(i < nt - 1, nxt, zero8)
        for r0 in range(0, tq, ROW_CHUNK):
            lo = F32_ROWS + r0
            conv = (cvs_ref[lo - 1:lo - 1 + ROW_CHUNK, cols] * cw_ref[0:1, cols]
                    + cvs_ref[lo:lo + ROW_CHUNK, cols] * cw_ref[1:2, cols]
                    + cvs_ref[lo + 1:lo + 1 + ROW_CHUNK, cols] * cw_ref[2:3, cols])
            ya_ref[r0:r0 + ROW_CHUNK, cols] = (
                z_ref[r0:r0 + ROW_CHUNK, zb].astype(F32) * conv).astype(BF16)


def _attn_conv(z, cos_t, sin_t, sink, conv_w):
    s = z.shape[0]
    tq = MIX_TQ
    nt = s // tq
    kvb = tq // BLOCK
    cb = tq // BF16_ROWS
    qbcv = 4 * D_CONV
    kv_w = 2 * D_KV
    assert OFF_B == 0 and OFF_Q + D_ATTN == qbcv and OFF_K % kv_w == 0 and OFF_V == OFF_K + D_KV

    def prev_blk(i, per):
        return jnp.maximum(i * per - 1, 0)

    def next_blk(i, per, total):
        return jnp.minimum((i + 1) * per, total - 1)

    in_specs = [
        pl.BlockSpec(memory_space=pltpu.SMEM),
        pl.BlockSpec((tq, qbcv), lambda i: (i, 0)),
        pl.BlockSpec((tq, kv_w), lambda i: (i, OFF_K // kv_w)),
        pl.BlockSpec((BLOCK, kv_w), lambda i: (prev_blk(i, kvb), OFF_K // kv_w)),
        pl.BlockSpec((BLOCK, kv_w), lambda i: (next_blk(i, kvb, s // BLOCK), OFF_K // kv_w)),
        pl.BlockSpec((BF16_ROWS, D_CONV), lambda i: (prev_blk(i, cb), OFF_C // D_CONV)),
        pl.BlockSpec((BF16_ROWS, D_CONV), lambda i: (prev_blk(i, cb), OFF_VA // D_CONV)),
        pl.BlockSpec((BF16_ROWS, D_CONV), lambda i: (next_blk(i, cb, s // BF16_ROWS), OFF_C // D_CONV)),
        pl.BlockSpec((BF16_ROWS, D_CONV), lambda i: (next_blk(i, cb, s // BF16_ROWS), OFF_VA // D_CONV)),
        pl.BlockSpec((tq, LANES), lambda i: (i, 0)),
        pl.BlockSpec((tq, LANES), lambda i: (i, 0)),
        pl.BlockSpec((BLOCK, LANES), lambda i: (prev_blk(i, kvb), 0)),
        pl.BlockSpec((BLOCK, LANES), lambda i: (prev_blk(i, kvb), 0)),
        pl.BlockSpec((BLOCK, LANES), lambda i: (next_blk(i, kvb, s // BLOCK), 0)),
        pl.BlockSpec((BLOCK, LANES), lambda i: (next_blk(i, kvb, s // BLOCK), 0)),
        pl.BlockSpec((3, D_CONV), lambda i: (0, 0)),
    ]
    r_ext = tq + 2 * BLOCK
    grp = N_HEADS // N_KV_HEADS
    return pl.pallas_call(
        functools.partial(_attn_conv_kernel, seq_len=s),
        grid=(nt,),
        in_specs=in_specs,
        out_specs=[pl.BlockSpec((tq, D_ATTN), lambda i: (i, 0)),
                   pl.BlockSpec((tq, D_CONV), lambda i: (i, 0))],
        out_shape=[jax.ShapeDtypeStruct((s, D_ATTN), BF16),
                   jax.ShapeDtypeStruct((s, D_CONV), BF16)],
        scratch_shapes=[
            pltpu.VMEM((N_KV_HEADS, tq // BLOCK, grp * BLOCK, LANES), BF16),
            pltpu.VMEM((N_KV_HEADS, r_ext, LANES), BF16),
            pltpu.VMEM((N_KV_HEADS, LANES, r_ext), BF16),
            pltpu.VMEM((tq + 2 * F32_ROWS, D_CONV), F32),
        ],
        compiler_params=pltpu.CompilerParams(
            dimension_semantics=("arbitrary",),
            vmem_limit_bytes=VMEM_LIMIT_BYTES),
        name="attn_conv",
    )(sink, z, z, z, z, z, z, z, z, cos_t, sin_t, cos_t, sin_t, cos_t, sin_t, conv_w)


def _merge_kernel(x_ref, ya_ref, att_ref, *rest):
    tm, d = x_ref.shape
    n_col = d // COL_CHUNK
    gla_refs, glb_refs = rest[:n_col], rest[n_col:2 * n_col]
    bg_ref, woa_ref, wo_ref, wmix_ref, h1_ref, ys_a_ref, ys_b_ref, mix_ref = rest[2 * n_col:]
    for c in range(n_col):
        cols = slice(c * COL_CHUNK, (c + 1) * COL_CHUNK)
        gcols = slice(d + c * COL_CHUNK, d + (c + 1) * COL_CHUNK)
        ys_a_ref[c % 2] = jnp.dot(ya_ref[...], woa_ref[:, cols], preferred_element_type=F32)
        ys_b_ref[c % 2] = jnp.dot(att_ref[...], wo_ref[:, cols], preferred_element_type=F32)
        for r0 in range(0, tm, ROW_CHUNK):
            rows = slice(r0, r0 + ROW_CHUNK)
            g_a = jax.nn.sigmoid(gla_refs[c][rows, :].astype(F32) + bg_ref[:, cols])
            g_b = jax.nn.sigmoid(glb_refs[c][rows, :].astype(F32) + bg_ref[:, gcols])
            mix_ref[rows, cols] = (g_a * ys_a_ref[c % 2, rows, :]
                                   + g_b * ys_b_ref[c % 2, rows, :]).astype(BF16)
    for c in range(d // COL_CHUNK):
        cols = slice(c * COL_CHUNK, (c + 1) * COL_CHUNK)
        h1_ref[:, cols] = x_ref[:, cols] + jnp.dot(mix_ref[...], wmix_ref[:, cols],
                                                   preferred_element_type=F32)


def _merge(x, ya, att, z, b_gate, w_out_a, w_o, w_mix):
    s, d = x.shape
    tm = MERGE_TM
    assert OFF_GA % COL_CHUNK == 0 and OFF_GB % COL_CHUNK == 0
    n_col = d // COL_CHUNK
    const = lambda i: (0, 0)
    single = dict(pipeline_mode=pl.Buffered(1))

    def gate_spec(off, c):
        return pl.BlockSpec((tm, COL_CHUNK), lambda i: (i, off // COL_CHUNK + c))

    in_specs = [
        pl.BlockSpec((tm, d), lambda i: (i, 0)),
        pl.BlockSpec((tm, D_CONV), lambda i: (i, 0)),
        pl.BlockSpec((tm, D_ATTN), lambda i: (i, 0)),
    ] + [gate_spec(OFF_GA, c) for c in range(n_col)] + [
        gate_spec(OFF_GB, c) for c in range(n_col)] + [
        pl.BlockSpec((1, 2 * d), const),
        pl.BlockSpec((D_CONV, d), const, **single),
        pl.BlockSpec((D_ATTN, d), const, **single),
        pl.BlockSpec((d, d), const, **single),
    ]
    return pl.pallas_call(
        _merge_kernel,
        grid=(s // tm,),
        in_specs=in_specs,
        out_specs=pl.BlockSpec((tm, d), lambda i: (i, 0)),
        out_shape=jax.ShapeDtypeStruct((s, d), F32),
        scratch_shapes=[
            pltpu.VMEM((2, tm, COL_CHUNK), F32),
            pltpu.VMEM((2, tm, COL_CHUNK), F32),
            pltpu.VMEM((tm, d), BF16),
        ],
        compiler_params=pltpu.CompilerParams(
            dimension_semantics=("arbitrary",),
            vmem_limit_bytes=VMEM_LIMIT_BYTES),
        name="merge",
    )(x, ya, att, *([z] * (2 * n_col)), b_gate, w_out_a, w_o, w_mix)


def _ffn_kernel(h_ref, hp_ref, hn_ref, g2_ref, wa_ref, wg_ref, cwa_ref, cwg_ref,
                ba_ref, bgt_ref, wd_ref, gf_ref,
                o_ref,
                u_ref, *sub_refs, final_norm):
    i = pl.program_id(0)
    j = pl.program_id(1)
    nt = pl.num_programs(0)
    nf = pl.num_programs(1)
    tm = h_ref.shape[0]
    halo = BF16_ROWS
    n_sub = len(sub_refs) // 3
    sa_refs, sg_refs, act_refs = sub_refs[0::3], sub_refs[1::3], sub_refs[2::3]

    @pl.when(j == 0)
    def _():
        g2 = g2_ref[...]
        zero = jnp.zeros((halo, h_ref.shape[1]), F32)
        u_ref[0:halo, :] = jnp.where(i > 0, _rmsnorm(hp_ref[...], g2), zero).astype(BF16)
        u_ref[halo + tm:2 * halo + tm, :] = jnp.where(
            i < nt - 1, _rmsnorm(hn_ref[...], g2), zero).astype(BF16)

        def body(r, carry):
            r0 = pl.multiple_of(r * ROW_CHUNK, ROW_CHUNK)
            u_ref[pl.ds(halo + r0, ROW_CHUNK), :] = _rmsnorm(
                h_ref[pl.ds(r0, ROW_CHUNK), :], g2).astype(BF16)
            o_ref[pl.ds(r0, ROW_CHUNK), :] = jnp.zeros((ROW_CHUNK, o_ref.shape[1]), F32)
            return carry
        lax.fori_loop(0, tm // ROW_CHUNK, body, 0)

    def conv(s_ref, cw_ref, b_ref, r0, cs):
        return (s_ref[r0 + halo - 1:r0 + halo - 1 + FFN_ACT_ROWS, :] * cw_ref[0:1, cs]
                + s_ref[r0 + halo:r0 + halo + FFN_ACT_ROWS, :] * cw_ref[1:2, cs]
                + s_ref[r0 + halo + 1:r0 + halo + 1 + FFN_ACT_ROWS, :] * cw_ref[2:3, cs]) + b_ref[:, cs]

    def sub_cols(k):
        return slice(k * FFN_SUB, (k + 1) * FFN_SUB)

    def up(k, w_ref, s_refs):
        s_refs[k][...] = jnp.dot(u_ref[...], w_ref[:, sub_cols(k)], preferred_element_type=F32)

    def activate(k, rb):
        for r0 in range(rb * FFN_DOWN_ROWS, (rb + 1) * FFN_DOWN_ROWS, FFN_ACT_ROWS):
            a = conv(sa_refs[k], cwa_ref, ba_ref, r0, sub_cols(k))
            g = conv(sg_refs[k], cwg_ref, bgt_ref, r0, sub_cols(k))
            act_refs[k][r0:r0 + FFN_ACT_ROWS, :] = (a * jax.nn.sigmoid(a) * g).astype(BF16)

    def down(k, rb):
        rows = slice(rb * FFN_DOWN_ROWS, (rb + 1) * FFN_DOWN_ROWS)
        for c in range(o_ref.shape[1] // COL_CHUNK):
            cols = slice(c * COL_CHUNK, (c + 1) * COL_CHUNK)
            o_ref[rows, cols] += jnp.dot(act_refs[k][rows, :], wd_ref[sub_cols(k), cols],
                                         preferred_element_type=F32)

    for k in range(n_sub):
        up(k, wa_ref, sa_refs)
        up(k, wg_ref, sg_refs)
    for k in range(n_sub):
        for rb in range(tm // FFN_DOWN_ROWS):
            activate(k, rb)
            down(k, rb)

    @pl.when(j == nf - 1)
    def _():
        def body(r, carry):
            rows = pl.ds(pl.multiple_of(r * ROW_CHUNK, ROW_CHUNK), ROW_CHUNK)
            h = h_ref[rows, :] + o_ref[rows, :]
            if final_norm:
                h = _rmsnorm(h, gf_ref[...])
            o_ref[rows, :] = h
            return carry
        lax.fori_loop(0, tm // ROW_CHUNK, body, 0)


def _ffn(h, g2, w_up, conv_w, conv_b, w_down, gf, final_norm):
    s, d = h.shape
    tm, tf = FFN_TM, FFN_TF
    nt, nf = s // tm, D_FF // tf
    hb = tm // BF16_ROWS
    in_specs = [
        pl.BlockSpec((tm, d), lambda i, j: (i, 0)),
        pl.BlockSpec((BF16_ROWS, d), lambda i, j: (jnp.maximum(i * hb - 1, 0), 0)),
        pl.BlockSpec((BF16_ROWS, d), lambda i, j: (jnp.minimum((i + 1) * hb, s // BF16_ROWS - 1), 0)),
        pl.BlockSpec((1, d), lambda i, j: (0, 0)),
        pl.BlockSpec((d, tf), lambda i, j: (0, j)),
        pl.BlockSpec((d, tf), lambda i, j: (0, nf + j)),
        pl.BlockSpec((3, tf), lambda i, j: (0, j)),
        pl.BlockSpec((3, tf), lambda i, j: (0, nf + j)),
        pl.BlockSpec((1, tf), lambda i, j: (0, j)),
        pl.BlockSpec((1, tf), lambda i, j: (0, nf + j)),
        pl.BlockSpec((tf, d), lambda i, j: (j, 0)),
        pl.BlockSpec((1, d), lambda i, j: (0, 0)),
    ]
    return pl.pallas_call(
        functools.partial(_ffn_kernel, final_norm=final_norm),
        grid=(nt, nf),
        in_specs=in_specs,
        out_specs=pl.BlockSpec((tm, d), lambda i, j: (i, 0)),
        out_shape=jax.ShapeDtypeStruct((s, d), F32),
        scratch_shapes=[
            pltpu.VMEM((tm + 2 * BF16_ROWS, d), BF16),
        ] + [
            pltpu.VMEM((tm + 2 * BF16_ROWS, FFN_SUB), F32),
            pltpu.VMEM((tm + 2 * BF16_ROWS, FFN_SUB), F32),
            pltpu.VMEM((tm, FFN_SUB), BF16),
        ] * (tf // FFN_SUB),
        compiler_params=pltpu.CompilerParams(
            dimension_semantics=("arbitrary", "arbitrary"),
            vmem_limit_bytes=VMEM_LIMIT_BYTES),
        name="ffn",
    )(h, h, h, g2, w_up, w_up, conv_w, conv_w, conv_b, conv_b, w_down, gf)


def _rope_tables(seq_len):
    half = HEAD_DIM // 2
    inv_freq = ROPE_THETA ** (-jnp.arange(0, half, dtype=F32) / half)
    reps = LANES // half
    ang = jnp.arange(seq_len, dtype=F32)[:, None] * jnp.tile(inv_freq, reps)[None, :]
    sign = jnp.tile(jnp.concatenate([-jnp.ones((half,), F32), jnp.ones((half,), F32)]), reps // 2)
    return jnp.cos(ang), jnp.sin(ang) * sign[None, :]


def kernel(x, norm_mix_g, w_in, b_gate, conv_a_w, w_out_a, sink_logits, w_o_attn, w_mix_out,
           norm_ffn_g, ffn_w_up, ffn_conv_w, ffn_conv_b, ffn_w_down, norm_final_g):
    b, s, d = x.shape
    depth = w_in.shape[0]
    cos_t, sin_t = _rope_tables(s)
    outs = []
    for bi in range(b):
        h = x[bi]
        for l in range(depth):
            z = _in_proj(h, norm_mix_g[l][None, :], w_in[l].astype(BF16))
            att, ya = _attn_conv(z, cos_t, sin_t, sink_logits[l], conv_a_w[l])
            h = _merge(h, ya, att, z, b_gate[l][None, :], w_out_a[l].astype(BF16),
                       w_o_attn[l].astype(BF16), w_mix_out[l].astype(BF16))
            h = _ffn(h, norm_ffn_g[l][None, :], ffn_w_up[l].astype(BF16), ffn_conv_w[l],
                     ffn_conv_b[l][None, :], ffn_w_down[l].astype(BF16),
                     norm_final_g[None, :], final_norm=(l == depth - 1))
        outs.append(h[None])
    return outs[0] if b == 1 else jnp.concatenate(outs, axis=0)
```

```python
import functools

import jax
import jax.numpy as jnp
from jax import lax
from jax.experimental import pallas as pl
from jax.experimental.pallas import tpu as pltpu

D_MODEL = 2048
D_CONV = D_MODEL // 2
N_HEADS = 16
N_KV_HEADS = 4
HEAD_DIM = 64
D_ATTN = N_HEADS * HEAD_DIM
D_KV = N_KV_HEADS * HEAD_DIM
WINDOW = 128
BLOCK = 128
ROPE_THETA = 10000.0
D_FF = 5632
EPS = 1e-6

OFF_B = 0
OFF_C = OFF_B + D_CONV
OFF_VA = OFF_C + D_CONV
OFF_Q = OFF_VA + D_CONV
OFF_K = OFF_Q + D_ATTN
OFF_V = OFF_K + D_KV
OFF_GA = OFF_V + D_KV
OFF_GB = OFF_GA + D_MODEL
D_IN_PROJ = OFF_GB + D_MODEL
LOG2E = 1.4426950408889634

LANES = 128
BF16_ROWS = 16
F32_ROWS = 8
VMEM_LIMIT_BYTES = 60000 * 1024
IN_VMEM_LIMIT_BYTES = 52 * 1024 * 1024
ATTN_VMEM_LIMIT_BYTES = 40 * 1024 * 1024

IN_TM = 1024
IN_TN = 2176
MIX_TQ = 512
MERGE_TM = 512
FFN_TM = 1024
FFN_TF = 512
FFN_SUB = 256
FFN_ACT_ROWS = 128
ROW_CHUNK = 128
COL_CHUNK = 512

BF16 = jnp.bfloat16
F32 = jnp.float32


def _rmsnorm(x, g):
    ms = jnp.mean(x * x, axis=-1, keepdims=True)
    return x * lax.rsqrt(ms + EPS) * g


def _in_proj_kernel(x_ref, g_ref, w_ref, *rest, n_cast):
    src_refs = rest[:n_cast]
    z_ref = rest[n_cast]
    dst_refs = rest[n_cast + 1:2 * n_cast + 1]
    u_ref = rest[2 * n_cast + 1]

    @pl.when(pl.program_id(1) == 0)
    def _():
        def body(r, carry):
            rows = pl.ds(pl.multiple_of(r * ROW_CHUNK, ROW_CHUNK), ROW_CHUNK)
            u_ref[rows, :] = _rmsnorm(x_ref[rows, :], g_ref[...]).astype(BF16)
            return carry
        lax.fori_loop(0, x_ref.shape[0] // ROW_CHUNK, body, 0)

    for src, dst in zip(src_refs, dst_refs):
        dst[...] = src[...].astype(dst.dtype)

    z_ref[...] = jnp.dot(u_ref[...], w_ref[...],
                         preferred_element_type=F32).astype(z_ref.dtype)


def _slab_spec(shape, n_steps, n_inner):
    rows, cols = shape
    per = rows // n_steps
    hold = 1
    while per % BF16_ROWS:
        per, hold = per * 2, hold * 2
    assert per * (n_steps // hold) == rows
    if n_inner is None:
        return pl.BlockSpec((per, cols), lambda i: (i // hold, 0))
    return pl.BlockSpec((per, cols), lambda i, j: ((i * n_inner + j) // hold, 0))


def _in_proj(x, g, w, cast_weights):
    s, d = x.shape
    n = w.shape[1]
    grid = (s // IN_TM, n // IN_TN)
    slabs = [_slab_spec(cw.shape, grid[0] * grid[1], grid[1]) for cw in cast_weights]
    outs = pl.pallas_call(
        functools.partial(_in_proj_kernel, n_cast=len(cast_weights)),
        grid=grid,
        in_specs=[
            pl.BlockSpec((IN_TM, d), lambda i, j: (i, 0)),
            pl.BlockSpec((1, d), lambda i, j: (0, 0)),
            pl.BlockSpec((d, IN_TN), lambda i, j: (0, j)),
        ] + slabs,
        out_specs=[pl.BlockSpec((IN_TM, IN_TN), lambda i, j: (i, j))] + slabs,
        out_shape=[jax.ShapeDtypeStruct((s, n), BF16)]
        + [jax.ShapeDtypeStruct(cw.shape, BF16) for cw in cast_weights],
        scratch_shapes=[pltpu.VMEM((IN_TM, d), BF16)],
        compiler_params=pltpu.CompilerParams(
            dimension_semantics=("arbitrary", "arbitrary"),
            vmem_limit_bytes=IN_VMEM_LIMIT_BYTES),
        name="in_proj",
    )(x, g, w, *cast_weights)
    return outs[0], outs[1:]


def _rope(x, cos, sin_signed, first_half):
    partner = jnp.where(first_half, pltpu.roll(x, LANES - HEAD_DIM // 2, 1),
                        pltpu.roll(x, HEAD_DIM // 2, 1))
    return x * cos + partner * sin_signed


def _attn_conv_kernel(sink_ref, z_ref, kv_ref, kvp_ref, kvn_ref, cp_ref, vp_ref, cn_ref, vn_ref,
                      cos_ref, sin_ref, cosp_ref, sinp_ref, cosn_ref, sinn_ref, cw_ref,
                      *rest, seq_len, n_cast):
    src_refs = rest[:n_cast]
    att_ref, ya_ref = rest[n_cast:n_cast + 2]
    dst_refs = rest[n_cast + 2:2 * n_cast + 2]
    qst_ref, kz_ref, vt_ref, cvs_ref = rest[2 * n_cast + 2:]
    for src, dst in zip(src_refs, dst_refs):
        dst[...] = src[...].astype(dst.dtype)

    i = pl.program_id(0)
    nt = pl.num_programs(0)
    tq = z_ref.shape[0]
    nb = tq // BLOCK
    grp = N_HEADS // N_KV_HEADS
    lane = lax.broadcasted_iota(jnp.int32, (1, LANES), 1)
    first_half = (lane % HEAD_DIM) < (HEAD_DIM // 2)
    low_head = lane < HEAD_DIM
    scale = HEAD_DIM ** -0.5 * LOG2E

    for b in range(nb):
        rows = slice(b * BLOCK, (b + 1) * BLOCK)
        cos_b = cos_ref[rows, :]
        sin_b = sin_ref[rows, :]
        for g in range(D_ATTN // LANES):
            h = g // 2
            xq = z_ref[rows, OFF_Q + g * LANES:OFF_Q + (g + 1) * LANES].astype(F32)
            xr = _rope(xq, cos_b, sin_b, first_half) * scale
            rolled = pltpu.roll(xr, HEAD_DIM, 1)
            even, odd = (xr, rolled) if h % 2 == 0 else (rolled, xr)
            j0 = 2 * (g % 2)
            qst_ref[h, b, j0 * BLOCK:(j0 + 1) * BLOCK, :] = even.astype(BF16)
            qst_ref[h, b, (j0 + 1) * BLOCK:(j0 + 2) * BLOCK, :] = odd.astype(BF16)

    sub = lax.broadcasted_iota(jnp.int32, (HEAD_DIM, BLOCK), 0)
    ones_rows = jnp.where(sub == 0, 1.0, 0.0).astype(F32)

    def put_kv(kv_blk_ref, c_ref, s_ref, blk, row0):
        src = slice(blk * BLOCK, (blk + 1) * BLOCK)
        dst = slice(row0, row0 + BLOCK)
        c = c_ref[src, :]
        s = s_ref[src, :]
        for pr in range(N_KV_HEADS // 2):
            kf = _rope(kv_blk_ref[src, pr * LANES:(pr + 1) * LANES].astype(F32), c, s, first_half)
            zero = jnp.zeros_like(kf)
            kz_ref[2 * pr, dst, :] = jnp.where(low_head, kf, zero).astype(BF16)
            kz_ref[2 * pr + 1, dst, :] = jnp.where(low_head, zero, kf).astype(BF16)
            vb_t = kv_blk_ref[src, D_KV + pr * LANES:D_KV + (pr + 1) * LANES].astype(F32).T
            vt_ref[2 * pr, :, dst] = jnp.concatenate(
                [vb_t[:HEAD_DIM], ones_rows], axis=0).astype(BF16)
            vt_ref[2 * pr + 1, :, dst] = jnp.concatenate(
                [vb_t[HEAD_DIM:], ones_rows], axis=0).astype(BF16)

    put_kv(kvp_ref, cosp_ref, sinp_ref, 0, 0)
    for blk in range(nb):
        put_kv(kv_ref, cos_ref, sin_ref, blk, (blk + 1) * BLOCK)
    put_kv(kvn_ref, cosn_ref, sinn_ref, 0, BLOCK + tq)

    c_io = lax.broadcasted_iota(jnp.int32, (BLOCK, BLOCK), 0)
    r_io = lax.broadcasted_iota(jnp.int32, (BLOCK, BLOCK), 1)

    def scores(b, h):
        return lax.dot_general(kz_ref[h, b * BLOCK:(b + 3) * BLOCK, :], qst_ref[h, b],
                               (((1,), (1,)), ((), ())),
                               preferred_element_type=F32)

    def finish(b, h, s_t):
        base = i * tq + b * BLOCK
        mask_lo = (c_io >= r_io) & (c_io + (base - BLOCK) >= 0)
        mask_hi = (c_io <= r_io) & (c_io + (base + BLOCK) < seq_len)
        probs, sink_terms = [], []
        for j in range(grp):
            sink = sink_ref[grp * h + j] * LOG2E
            cols = slice(j * BLOCK, (j + 1) * BLOCK)
            lo = jnp.where(mask_lo, s_t[0:BLOCK, cols], -jnp.inf)
            mid = s_t[BLOCK:2 * BLOCK, cols]
            hi = jnp.where(mask_hi, s_t[2 * BLOCK:3 * BLOCK, cols], -jnp.inf)
            m = jnp.maximum(jnp.maximum(jnp.max(lo, axis=0, keepdims=True),
                                        jnp.max(hi, axis=0, keepdims=True)),
                            jnp.maximum(jnp.max(mid, axis=0, keepdims=True), sink))
            probs.append(jnp.concatenate(
                [jnp.exp2(lo - m), jnp.exp2(mid - m), jnp.exp2(hi - m)], axis=0).astype(BF16))
            sink_terms.append(jnp.exp2(sink - m))
        p_t = jnp.concatenate(probs, axis=1)
        o_t = jnp.dot(vt_ref[h, :, b * BLOCK:(b + 3) * BLOCK], p_t,
                      preferred_element_type=F32)
        outs = []
        for j in range(grp):
            cols = slice(j * BLOCK, (j + 1) * BLOCK)
            denom = o_t[HEAD_DIM:HEAD_DIM + 1, cols] + sink_terms[j]
            outs.append(o_t[:HEAD_DIM, cols] * (1.0 / denom))
        for pr in range(grp // 2):
            pair_t = jnp.concatenate([outs[2 * pr], outs[2 * pr + 1]], axis=0)
            g = (grp // 2) * h + pr
            att_ref[b * BLOCK:(b + 1) * BLOCK, g * LANES:(g + 1) * LANES] = pair_t.T.astype(BF16)

    pending = None
    for b in range(nb):
        for h in range(N_KV_HEADS):
            s_t = scores(b, h)
            if pending is not None:
                finish(*pending)
            pending = (b, h, s_t)
    finish(*pending)

    zero8 = jnp.zeros((F32_ROWS, LANES), F32)
    for g in range(D_CONV // LANES):
        cols = slice(g * LANES, (g + 1) * LANES)
        zc = slice(OFF_C + g * LANES, OFF_C + (g + 1) * LANES)
        zv = slice(OFF_VA + g * LANES, OFF_VA + (g + 1) * LANES)
        zb = slice(OFF_B + g * LANES, OFF_B + (g + 1) * LANES)
        prev = (cp_ref[:, cols].astype(F32) * vp_ref[:, cols].astype(F32))[F32_ROWS:, :]
        nxt = (cn_ref[:, cols].astype(F32) * vn_ref[:, cols].astype(F32))[:F32_ROWS, :]
        cvs_ref[0:F32_ROWS, cols] = jnp.where(i > 0, prev, zero8)
        for r0 in range(0, tq, ROW_CHUNK):
            rows = slice(r0, r0 + ROW_CHUNK)
            cvs_ref[F32_ROWS + r0:F32_ROWS + r0 + ROW_CHUNK, cols] = (
                z_ref[rows, zc].astype(F32) * z_ref[rows, zv].astype(F32))
        cvs_ref[F32_ROWS + tq:2 * F32_ROWS + tq, cols] = jnp.where(i < nt - 1, nxt, zero8)
        for r0 in range(0, tq, ROW_CHUNK):
            lo = F32_ROWS + r0
            conv = (cvs_ref[lo - 1:lo - 1 + ROW_CHUNK, cols] * cw_ref[0:1, cols]
                    + cvs_ref[lo:lo + ROW_CHUNK, cols] * cw_ref[1:2, cols]
                    + cvs_ref[lo + 1:lo + 1 + ROW_CHUNK, cols] * cw_ref[2:3, cols])
            ya_ref[r0:r0 + ROW_CHUNK, cols] = (
                z_ref[r0:r0 + ROW_CHUNK, zb].astype(F32) * conv).astype(BF16)


def _attn_conv(z, cos_t, sin_t, sink, conv_w, cast_weights):
    s = z.shape[0]
    tq = MIX_TQ
    nt = s // tq
    kvb = tq // BLOCK
    cb = tq // BF16_ROWS
    qbcv = 4 * D_CONV
    kv_w = 2 * D_KV
    assert OFF_B == 0 and OFF_Q + D_ATTN == qbcv and OFF_K % kv_w == 0 and OFF_V == OFF_K + D_KV

    def prev_blk(i, per):
        return jnp.maximum(i * per - 1, 0)

    def next_blk(i, per, total):
        return jnp.minimum((i + 1) * per, total - 1)

    in_specs = [
        pl.BlockSpec(memory_space=pltpu.SMEM),
        pl.BlockSpec((tq, qbcv), lambda i: (i, 0)),
        pl.BlockSpec((tq, kv_w), lambda i: (i, OFF_K // kv_w)),
        pl.BlockSpec((BLOCK, kv_w), lambda i: (prev_blk(i, kvb), OFF_K // kv_w)),
        pl.BlockSpec((BLOCK, kv_w), lambda i: (next_blk(i, kvb, s // BLOCK), OFF_K // kv_w)),
        pl.BlockSpec((BF16_ROWS, D_CONV), lambda i: (prev_blk(i, cb), OFF_C // D_CONV)),
        pl.BlockSpec((BF16_ROWS, D_CONV), lambda i: (prev_blk(i, cb), OFF_VA // D_CONV)),
        pl.BlockSpec((BF16_ROWS, D_CONV), lambda i: (next_blk(i, cb, s // BF16_ROWS), OFF_C // D_CONV)),
        pl.BlockSpec((BF16_ROWS, D_CONV), lambda i: (next_blk(i, cb, s // BF16_ROWS), OFF_VA // D_CONV)),
        pl.BlockSpec((tq, LANES), lambda i: (i, 0)),
        pl.BlockSpec((tq, LANES), lambda i: (i, 0)),
        pl.BlockSpec((BLOCK, LANES), lambda i: (prev_blk(i, kvb), 0)),
        pl.BlockSpec((BLOCK, LANES), lambda i: (prev_blk(i, kvb), 0)),
        pl.BlockSpec((BLOCK, LANES), lambda i: (next_blk(i, kvb, s // BLOCK), 0)),
        pl.BlockSpec((BLOCK, LANES), lambda i: (next_blk(i, kvb, s // BLOCK), 0)),
        pl.BlockSpec((3, D_CONV), lambda i: (0, 0)),
    ]
    r_ext = tq + 2 * BLOCK
    grp = N_HEADS // N_KV_HEADS
    slabs = [_slab_spec(cw.shape, nt, None) for cw in cast_weights]
    outs = pl.pallas_call(
        functools.partial(_attn_conv_kernel, seq_len=s, n_cast=len(cast_weights)),
        grid=(nt,),
        in_specs=in_specs + slabs,
        out_specs=[pl.BlockSpec((tq, D_ATTN), lambda i: (i, 0)),
                   pl.BlockSpec((tq, D_CONV), lambda i: (i, 0))] + slabs,
        out_shape=[jax.ShapeDtypeStruct((s, D_ATTN), BF16),
                   jax.ShapeDtypeStruct((s, D_CONV), BF16)]
        + [jax.ShapeDtypeStruct(cw.shape, BF16) for cw in cast_weights],
        scratch_shapes=[
            pltpu.VMEM((N_KV_HEADS, tq // BLOCK, grp * BLOCK, LANES), BF16),
            pltpu.VMEM((N_KV_HEADS, r_ext, LANES), BF16),
            pltpu.VMEM((N_KV_HEADS, LANES, r_ext), BF16),
            pltpu.VMEM((tq + 2 * F32_ROWS, D_CONV), F32),
        ],
        compiler_params=pltpu.CompilerParams(
            dimension_semantics=("arbitrary",),
            vmem_limit_bytes=ATTN_VMEM_LIMIT_BYTES),
        name="attn_conv",
    )(sink, z, z, z, z, z, z, z, z, cos_t, sin_t, cos_t, sin_t, cos_t, sin_t, conv_w,
      *cast_weights)
    return outs[0], outs[1], outs[2:]


def _merge_kernel(x_ref, ya_ref, att_ref, *rest):
    tm, d = x_ref.shape
    n_col = d // COL_CHUNK
    gla_refs, glb_refs = rest[:n_col], rest[n_col:2 * n_col]
    bg_ref, woa_ref, wo_ref, wmix_ref, h1_ref, ys_a_ref, ys_b_ref, mix_ref = rest[2 * n_col:]
    for c in range(n_col):
        cols = slice(c * COL_CHUNK, (c + 1) * COL_CHUNK)
        gcols = slice(d + c * COL_CHUNK, d + (c + 1) * COL_CHUNK)
        ys_a_ref[c % 2] = jnp.dot(ya_ref[...], woa_ref[:, cols], preferred_element_type=F32)
        ys_b_ref[c % 2] = jnp.dot(att_ref[...], wo_ref[:, cols], preferred_element_type=F32)
        for r0 in range(0, tm, ROW_CHUNK):
            rows = slice(r0, r0 + ROW_CHUNK)
            g_a = jax.nn.sigmoid(gla_refs[c][rows, :].astype(F32) + bg_ref[:, cols])
            g_b = jax.nn.sigmoid(glb_refs[c][rows, :].astype(F32) + bg_ref[:, gcols])
            mix_ref[rows, cols] = (g_a * ys_a_ref[c % 2, rows, :]
                                   + g_b * ys_b_ref[c % 2, rows, :]).astype(BF16)
    for c in range(d // COL_CHUNK):
        cols = slice(c * COL_CHUNK, (c + 1) * COL_CHUNK)
        h1_ref[:, cols] = x_ref[:, cols] + jnp.dot(mix_ref[...], wmix_ref[:, cols],
                                                   preferred_element_type=F32)


def _merge(x, ya, att, z, b_gate, w_out_a, w_o, w_mix):
    s, d = x.shape
    tm = MERGE_TM
    assert OFF_GA % COL_CHUNK == 0 and OFF_GB % COL_CHUNK == 0
    n_col = d // COL_CHUNK
    const = lambda i: (0, 0)
    single = dict(pipeline_mode=pl.Buffered(1))

    def gate_spec(off, c):
        return pl.BlockSpec((tm, COL_CHUNK), lambda i: (i, off // COL_CHUNK + c))

    in_specs = [
        pl.BlockSpec((tm, d), lambda i: (i, 0)),
        pl.BlockSpec((tm, D_CONV), lambda i: (i, 0)),
        pl.BlockSpec((tm, D_ATTN), lambda i: (i, 0)),
    ] + [gate_spec(OFF_GA, c) for c in range(n_col)] + [
        gate_spec(OFF_GB, c) for c in range(n_col)] + [
        pl.BlockSpec((1, 2 * d), const),
        pl.BlockSpec((D_CONV, d), const, **single),
        pl.BlockSpec((D_ATTN, d), const, **single),
        pl.BlockSpec((d, d), const, **single),
    ]
    return pl.pallas_call(
        _merge_kernel,
        grid=(s // tm,),
        in_specs=in_specs,
        out_specs=pl.BlockSpec((tm, d), lambda i: (i, 0)),
        out_shape=jax.ShapeDtypeStruct((s, d), F32),
        scratch_shapes=[
            pltpu.VMEM((2, tm, COL_CHUNK), F32),
            pltpu.VMEM((2, tm, COL_CHUNK), F32),
            pltpu.VMEM((tm, d), BF16),
        ],
        compiler_params=pltpu.CompilerParams(
            dimension_semantics=("arbitrary",),
            vmem_limit_bytes=VMEM_LIMIT_BYTES),
        name="merge",
    )(x, ya, att, *([z] * (2 * n_col)), b_gate, w_out_a, w_o, w_mix)


def _ffn_kernel(h_ref, hp_ref, hn_ref, g2_ref, wa_ref, wg_ref, cwa_ref, cwg_ref,
                ba_ref, bgt_ref, wd_ref, gf_ref,
                o_ref,
                u_ref, *sub_refs, final_norm):
    i = pl.program_id(0)
    j = pl.program_id(1)
    nt = pl.num_programs(0)
    nf = pl.num_programs(1)
    tm = h_ref.shape[0]
    halo = BF16_ROWS
    n_sub = len(sub_refs) // 3
    sa_refs, sg_refs, act_refs = sub_refs[0::3], sub_refs[1::3], sub_refs[2::3]

    @pl.when(j == 0)
    def _():
        g2 = g2_ref[...]
        zero = jnp.zeros((halo, h_ref.shape[1]), F32)
        u_ref[0:halo, :] = jnp.where(i > 0, _rmsnorm(hp_ref[...], g2), zero).astype(BF16)
        u_ref[halo + tm:2 * halo + tm, :] = jnp.where(
            i < nt - 1, _rmsnorm(hn_ref[...], g2), zero).astype(BF16)

        def body(r, carry):
            r0 = pl.multiple_of(r * ROW_CHUNK, ROW_CHUNK)
            u_ref[pl.ds(halo + r0, ROW_CHUNK), :] = _rmsnorm(
                h_ref[pl.ds(r0, ROW_CHUNK), :], g2).astype(BF16)
            o_ref[pl.ds(r0, ROW_CHUNK), :] = jnp.zeros((ROW_CHUNK, o_ref.shape[1]), F32)
            return carry
        lax.fori_loop(0, tm // ROW_CHUNK, body, 0)

    def conv(s_ref, cw_ref, b_ref, r0, cs):
        return (s_ref[r0 + halo - 1:r0 + halo - 1 + FFN_ACT_ROWS, :] * cw_ref[0:1, cs]
                + s_ref[r0 + halo:r0 + halo + FFN_ACT_ROWS, :] * cw_ref[1:2, cs]
                + s_ref[r0 + halo + 1:r0 + halo + 1 + FFN_ACT_ROWS, :] * cw_ref[2:3, cs]) + b_ref[:, cs]

    def sub_cols(k):
        return slice(k * FFN_SUB, (k + 1) * FFN_SUB)

    def up(k, w_ref, s_refs):
        s_refs[k][...] = jnp.dot(u_ref[...], w_ref[:, sub_cols(k)], preferred_element_type=F32)

    def activate(k):
        for r0 in range(0, tm, FFN_ACT_ROWS):
            a = conv(sa_refs[k], cwa_ref, ba_ref, r0, sub_cols(k))
            g = conv(sg_refs[k], cwg_ref, bgt_ref, r0, sub_cols(k))
            act_refs[k][r0:r0 + FFN_ACT_ROWS, :] = (a * jax.nn.sigmoid(a) * g).astype(BF16)

    def down(k):
        for c in range(o_ref.shape[1] // COL_CHUNK):
            cols = slice(c * COL_CHUNK, (c + 1) * COL_CHUNK)
            o_ref[:, cols] += jnp.dot(act_refs[k][...], wd_ref[sub_cols(k), cols],
                                      preferred_element_type=F32)

    for k in range(n_sub):
        up(k, wa_ref, sa_refs)
        up(k, wg_ref, sg_refs)
    for k in range(n_sub):
        activate(k)
        down(k)

    @pl.when(j == nf - 1)
    def _():
        def body(r, carry):
            rows = pl.ds(pl.multiple_of(r * ROW_CHUNK, ROW_CHUNK), ROW_CHUNK)
            h = h_ref[rows, :] + o_ref[rows, :]
            if final_norm:
                h = _rmsnorm(h, gf_ref[...])
            o_ref[rows, :] = h
            return carry
        lax.fori_loop(0, tm // ROW_CHUNK, body, 0)


def _ffn(h, g2, w_up, conv_w, conv_b, w_down, gf, final_norm):
    s, d = h.shape
    tm, tf = FFN_TM, FFN_TF
    nt, nf = s // tm, D_FF // tf
    hb = tm // BF16_ROWS
    in_specs = [
        pl.BlockSpec((tm, d), lambda i, j: (i, 0)),
        pl.BlockSpec((BF16_ROWS, d), lambda i, j: (jnp.maximum(i * hb - 1, 0), 0)),
        pl.BlockSpec((BF16_ROWS, d), lambda i, j: (jnp.minimum((i + 1) * hb, s // BF16_ROWS - 1), 0)),
        pl.BlockSpec((1, d), lambda i, j: (0, 0)),
        pl.BlockSpec((d, tf), lambda i, j: (0, j)),
        pl.BlockSpec((d, tf), lambda i, j: (0, nf + j)),
        pl.BlockSpec((3, tf), lambda i, j: (0, j)),
        pl.BlockSpec((3, tf), lambda i, j: (0, nf + j)),
        pl.BlockSpec((1, tf), lambda i, j: (0, j)),
        pl.BlockSpec((1, tf), lambda i, j: (0, nf + j)),
        pl.BlockSpec((tf, d), lambda i, j: (j, 0)),
        pl.BlockSpec((1, d), lambda i, j: (0, 0)),
    ]
    return pl.pallas_call(
        functools.partial(_ffn_kernel, final_norm=final_norm),
        grid=(nt, nf),
        in_specs=in_specs,
        out_specs=pl.BlockSpec((tm, d), lambda i, j: (i, 0)),
        out_shape=jax.ShapeDtypeStruct((s, d), F32),
        scratch_shapes=[
            pltpu.VMEM((tm + 2 * BF16_ROWS, d), BF16),
        ] + [
            pltpu.VMEM((tm + 2 * BF16_ROWS, FFN_SUB), F32),
            pltpu.VMEM((tm + 2 * BF16_ROWS, FFN_SUB), F32),
            pltpu.VMEM((tm, FFN_SUB), BF16),
        ] * (tf // FFN_SUB),
        compiler_params=pltpu.CompilerParams(
            dimension_semantics=("arbitrary", "arbitrary"),
            vmem_limit_bytes=VMEM_LIMIT_BYTES),
        name="ffn",
    )(h, h, h, g2, w_up, w_up, conv_w, conv_w, conv_b, conv_b, w_down, gf)


def _rope_tables(seq_len):
    half = HEAD_DIM // 2
    inv_freq = ROPE_THETA ** (-jnp.arange(0, half, dtype=F32) / half)
    reps = LANES // half
    ang = jnp.arange(seq_len, dtype=F32)[:, None] * jnp.tile(inv_freq, reps)[None, :]
    sign = jnp.tile(jnp.concatenate([-jnp.ones((half,), F32), jnp.ones((half,), F32)]), reps // 2)
    return jnp.cos(ang), jnp.sin(ang) * sign[None, :]


def kernel(x, norm_mix_g, w_in, b_gate, conv_a_w, w_out_a, sink_logits, w_o_attn, w_mix_out,
           norm_ffn_g, ffn_w_up, ffn_conv_w, ffn_conv_b, ffn_w_down, norm_final_g):
    b, s, d = x.shape
    depth = w_in.shape[0]
    cos_t, sin_t = _rope_tables(s)
    outs = []
    for bi in range(b):
        h = x[bi]
        for l in range(depth):
            z, (w_oa, w_o, w_mix) = _in_proj(
                h, norm_mix_g[l][None, :], w_in[l].astype(BF16),
                (w_out_a[l], w_o_attn[l], w_mix_out[l]))
            att, ya, (w_up, w_down) = _attn_conv(
                z, cos_t, sin_t, sink_logits[l], conv_a_w[l], (ffn_w_up[l], ffn_w_down[l]))
            h = _merge(h, ya, att, z, b_gate[l][None, :], w_oa, w_o, w_mix)
            h = _ffn(h, norm_ffn_g[l][None, :], w_up, ffn_conv_w[l], ffn_conv_b[l][None, :],
                     w_down, norm_final_g[None, :], final_norm=(l == depth - 1))
        outs.append(h[None])
    return outs[0] if b == 1 else jnp.concatenate(outs, axis=0)
```

```python
import functools

import jax
import jax.numpy as jnp
from jax import lax
from jax.experimental import pallas as pl
from jax.experimental.pallas import tpu as pltpu

D_MODEL = 2048
D_CONV = D_MODEL // 2
N_HEADS = 16
N_KV_HEADS = 4
HEAD_DIM = 64
D_ATTN = N_HEADS * HEAD_DIM
D_KV = N_KV_HEADS * HEAD_DIM
WINDOW = 128
BLOCK = 128
ROPE_THETA = 10000.0
D_FF = 5632
EPS = 1e-6

OFF_B = 0
OFF_C = OFF_B + D_CONV
OFF_VA = OFF_C + D_CONV
OFF_Q = OFF_VA + D_CONV
OFF_K = OFF_Q + D_ATTN
OFF_V = OFF_K + D_KV
OFF_GA = OFF_V + D_KV
OFF_GB = OFF_GA + D_MODEL
D_IN_PROJ = OFF_GB + D_MODEL
LOG2E = 1.4426950408889634

LANES = 128
BF16_ROWS = 16
F32_ROWS = 8
VMEM_LIMIT_BYTES = 60000 * 1024
IN_VMEM_LIMIT_BYTES = 52 * 1024 * 1024
MIX_VMEM_LIMIT_BYTES = 52 * 1024 * 1024

IN_TM = 1024
IN_TN = 2176
MIX_TQ = 256
MIX_OUT_COLS = 256
MIX_LEAD_UNITS = 0
FFN_TM = 1024
FFN_TF = 512
FFN_SUB = 256
FFN_ACT_ROWS = 128
ROW_CHUNK = 128
COL_CHUNK = 512

BF16 = jnp.bfloat16
F32 = jnp.float32


def _rmsnorm(x, g):
    ms = jnp.mean(x * x, axis=-1, keepdims=True)
    return x * lax.rsqrt(ms + EPS) * g


def _in_proj_kernel(x_ref, g_ref, w_ref, *rest, n_cast):
    src_refs = rest[:n_cast]
    z_ref = rest[n_cast]
    dst_refs = rest[n_cast + 1:2 * n_cast + 1]
    u_ref = rest[2 * n_cast + 1]

    @pl.when(pl.program_id(1) == 0)
    def _():
        def body(r, carry):
            rows = pl.ds(pl.multiple_of(r * ROW_CHUNK, ROW_CHUNK), ROW_CHUNK)
            u_ref[rows, :] = _rmsnorm(x_ref[rows, :], g_ref[...]).astype(BF16)
            return carry
        lax.fori_loop(0, x_ref.shape[0] // ROW_CHUNK, body, 0)

    for src, dst in zip(src_refs, dst_refs):
        dst[...] = src[...].astype(dst.dtype)

    z_ref[...] = jnp.dot(u_ref[...], w_ref[...],
                         preferred_element_type=F32).astype(z_ref.dtype)


def _slab_spec(shape, n_steps, step_fn):
    rows, cols = shape
    per = rows // n_steps
    hold = 1
    while per % BF16_ROWS:
        per, hold = per * 2, hold * 2
    assert per * (n_steps // hold) == rows
    return pl.BlockSpec((per, cols), lambda *idx: (step_fn(*idx) // hold, 0))


def _in_proj(x, g, w, cast_weights):
    s, d = x.shape
    n = w.shape[1]
    grid = (s // IN_TM, n // IN_TN)
    slabs = [_slab_spec(cw.shape, grid[0] * grid[1], lambda i, j: i * grid[1] + j)
             for cw in cast_weights]
    outs = pl.pallas_call(
        functools.partial(_in_proj_kernel, n_cast=len(cast_weights)),
        grid=grid,
        in_specs=[
            pl.BlockSpec((IN_TM, d), lambda i, j: (i, 0)),
            pl.BlockSpec((1, d), lambda i, j: (0, 0)),
            pl.BlockSpec((d, IN_TN), lambda i, j: (0, j)),
        ] + slabs,
        out_specs=[pl.BlockSpec((IN_TM, IN_TN), lambda i, j: (i, j))] + slabs,
        out_shape=[jax.ShapeDtypeStruct((s, n), BF16)]
        + [jax.ShapeDtypeStruct(cw.shape, BF16) for cw in cast_weights],
        scratch_shapes=[pltpu.VMEM((IN_TM, d), BF16)],
        compiler_params=pltpu.CompilerParams(
            dimension_semantics=("arbitrary", "arbitrary"),
            vmem_limit_bytes=IN_VMEM_LIMIT_BYTES),
        name="in_proj",
    )(x, g, w, *cast_weights)
    return outs[0], outs[1:]


def _rope(x, cos, sin_signed, first_half):
    partner = jnp.where(first_half, pltpu.roll(x, LANES - HEAD_DIM // 2, 1),
                        pltpu.roll(x, HEAD_DIM // 2, 1))
    return x * cos + partner * sin_signed


def _mixer_kernel(sink_ref, z_ref, kv_ref, kvp_ref, kvn_ref, cp_ref, vp_ref, cn_ref, vn_ref,
                  cos_ref, sin_ref, cosp_ref, sinp_ref, cosn_ref, sinn_ref, cw_ref, x_ref,
                  *rest, seq_len, n_cast):
    n_col = x_ref.shape[1] // COL_CHUNK
    gla_refs, glb_refs = rest[:n_col], rest[n_col:2 * n_col]
    bg_ref, woa_ref, wo_ref, wmix_ref = rest[2 * n_col:2 * n_col + 4]
    rest = rest[2 * n_col + 4:]
    src_refs = rest[:n_cast]
    h1_ref = rest[n_cast]
    dst_refs = rest[n_cast + 1:2 * n_cast + 1]
    (att_s_ref, ya_s_ref, qst_ref, kz_ref, vt_ref, cvs_ref,
     ys_a_ref, ys_b_ref, mix_ref) = rest[2 * n_cast + 1:]
    for src, dst in zip(src_refs, dst_refs):
        dst[...] = src[...].astype(dst.dtype)

    step = pl.program_id(0)
    nt = pl.num_programs(0) - 1
    i = jnp.minimum(step, nt - 1)
    slot = step % 2
    pslot = 1 - slot
    tq = z_ref.shape[0]
    d = x_ref.shape[1]

    @pl.when(step == 0)
    def _():
        att_s_ref[1] = jnp.zeros(att_s_ref.shape[1:], BF16)
        ya_s_ref[1] = jnp.zeros(ya_s_ref.shape[1:], BF16)

    def merge_stage1(c):
        cols = slice(c * COL_CHUNK, (c + 1) * COL_CHUNK)
        gcols = slice(d + c * COL_CHUNK, d + (c + 1) * COL_CHUNK)
        ys_a_ref[c % 2] = jnp.dot(ya_s_ref[pslot], woa_ref[:, cols], preferred_element_type=F32)
        ys_b_ref[c % 2] = jnp.dot(att_s_ref[pslot], wo_ref[:, cols], preferred_element_type=F32)
        for r0 in range(0, tq, ROW_CHUNK):
            rows = slice(r0, r0 + ROW_CHUNK)
            g_a = jax.nn.sigmoid(gla_refs[c][rows, :].astype(F32) + bg_ref[:, cols])
            g_b = jax.nn.sigmoid(glb_refs[c][rows, :].astype(F32) + bg_ref[:, gcols])
            mix_ref[rows, cols] = (g_a * ys_a_ref[c % 2, rows, :]
                                   + g_b * ys_b_ref[c % 2, rows, :]).astype(BF16)

    def merge_stage2(c):
        cols = slice(c * MIX_OUT_COLS, (c + 1) * MIX_OUT_COLS)
        h1_ref[:, cols] = x_ref[:, cols] + jnp.dot(mix_ref[...], wmix_ref[:, cols],
                                                   preferred_element_type=F32)

    for c in range(n_col):
        merge_stage1(c)
    merge_pieces = [functools.partial(merge_stage2, c) for c in range(d // MIX_OUT_COLS)]

    nb = tq // BLOCK
    grp = N_HEADS // N_KV_HEADS
    lane = lax.broadcasted_iota(jnp.int32, (1, LANES), 1)
    first_half = (lane % HEAD_DIM) < (HEAD_DIM // 2)
    low_head = lane < HEAD_DIM
    scale = HEAD_DIM ** -0.5 * LOG2E

    for b in range(nb):
        rows = slice(b * BLOCK, (b + 1) * BLOCK)
        cos_b = cos_ref[rows, :]
        sin_b = sin_ref[rows, :]
        for g in range(D_ATTN // LANES):
            h = g // 2
            xq = z_ref[rows, OFF_Q + g * LANES:OFF_Q + (g + 1) * LANES].astype(F32)
            xr = _rope(xq, cos_b, sin_b, first_half) * scale
            rolled = pltpu.roll(xr, HEAD_DIM, 1)
            even, odd = (xr, rolled) if h % 2 == 0 else (rolled, xr)
            j0 = 2 * (g % 2)
            qst_ref[h, b, j0 * BLOCK:(j0 + 1) * BLOCK, :] = even.astype(BF16)
            qst_ref[h, b, (j0 + 1) * BLOCK:(j0 + 2) * BLOCK, :] = odd.astype(BF16)

    sub = lax.broadcasted_iota(jnp.int32, (BF16_ROWS, BLOCK), 0)
    ones_rows = jnp.where(sub == 0, 1.0, 0.0).astype(F32)

    def put_kv(kv_blk_ref, c_ref, s_ref, blk, row0):
        src = slice(blk * BLOCK, (blk + 1) * BLOCK)
        dst = slice(row0, row0 + BLOCK)
        c = c_ref[src, :]
        s = s_ref[src, :]
        for pr in range(N_KV_HEADS // 2):
            kf = _rope(kv_blk_ref[src, pr * LANES:(pr + 1) * LANES].astype(F32), c, s, first_half)
            zero = jnp.zeros_like(kf)
            kz_ref[2 * pr, dst, :] = jnp.where(low_head, kf, zero).astype(BF16)
            kz_ref[2 * pr + 1, dst, :] = jnp.where(low_head, zero, kf).astype(BF16)
            vb_t = kv_blk_ref[src, D_KV + pr * LANES:D_KV + (pr + 1) * LANES].astype(F32).T
            vt_ref[2 * pr, :, dst] = jnp.concatenate(
                [vb_t[:HEAD_DIM], ones_rows], axis=0).astype(BF16)
            vt_ref[2 * pr + 1, :, dst] = jnp.concatenate(
                [vb_t[HEAD_DIM:], ones_rows], axis=0).astype(BF16)

    put_kv(kvp_ref, cosp_ref, sinp_ref, 0, 0)
    for blk in range(nb):
        put_kv(kv_ref, cos_ref, sin_ref, blk, (blk + 1) * BLOCK)
    put_kv(kvn_ref, cosn_ref, sinn_ref, 0, BLOCK + tq)

    c_io = lax.broadcasted_iota(jnp.int32, (BLOCK, BLOCK), 0)
    r_io = lax.broadcasted_iota(jnp.int32, (BLOCK, BLOCK), 1)

    def scores(b, h):
        return lax.dot_general(kz_ref[h, b * BLOCK:(b + 3) * BLOCK, :], qst_ref[h, b],
                               (((1,), (1,)), ((), ())),
                               preferred_element_type=F32)

    def finish(b, h, s_t):
        base = i * tq + b * BLOCK
        mask_lo = (c_io >= r_io) & (c_io + (base - BLOCK) >= 0)
        mask_hi = (c_io <= r_io) & (c_io + (base + BLOCK) < seq_len)
        probs, sink_terms = [], []
        for j in range(grp):
            sink = sink_ref[grp * h + j] * LOG2E
            cols = slice(j * BLOCK, (j + 1) * BLOCK)
            lo = jnp.where(mask_lo, s_t[0:BLOCK, cols], -jnp.inf)
            mid = s_t[BLOCK:2 * BLOCK, cols]
            hi = jnp.where(mask_hi, s_t[2 * BLOCK:3 * BLOCK, cols], -jnp.inf)
            m = jnp.maximum(jnp.maximum(jnp.max(lo, axis=0, keepdims=True),
                                        jnp.max(hi, axis=0, keepdims=True)),
                            jnp.maximum(jnp.max(mid, axis=0, keepdims=True), sink))
            probs.append(jnp.concatenate(
                [jnp.exp2(lo - m), jnp.exp2(mid - m), jnp.exp2(hi - m)], axis=0).astype(BF16))
            sink_terms.append(jnp.exp2(sink - m))
        p_t = jnp.concatenate(probs, axis=1)
        o_t = jnp.dot(vt_ref[h, :, b * BLOCK:(b + 3) * BLOCK], p_t,
                      preferred_element_type=F32)
        outs = []
        for j in range(grp):
            cols = slice(j * BLOCK, (j + 1) * BLOCK)
            denom = o_t[HEAD_DIM:HEAD_DIM + 1, cols] + sink_terms[j]
            outs.append(o_t[:HEAD_DIM, cols] * (1.0 / denom))
        for pr in range(grp // 2):
            pair_t = jnp.concatenate([outs[2 * pr], outs[2 * pr + 1]], axis=0)
            g = (grp // 2) * h + pr
            att_s_ref[slot, b * BLOCK:(b + 1) * BLOCK, g * LANES:(g + 1) * LANES] = (
                pair_t.T.astype(BF16))

    units = [(b, h) for b in range(nb) for h in range(N_KV_HEADS)]
    n_units, n_pieces = len(units), len(merge_pieces)
    lead = MIX_LEAD_UNITS
    s_next = scores(*units[0])
    for u, (b, h) in enumerate(units):
        s_t = s_next
        if u + 1 < n_units:
            s_next = scores(*units[u + 1])
        if u >= lead:
            for piece in merge_pieces[(u - lead) * n_pieces // (n_units - lead):
                                      (u - lead + 1) * n_pieces // (n_units - lead)]:
                piece()
        finish(b, h, s_t)

    zero8 = jnp.zeros((F32_ROWS, LANES), F32)
    for g in range(D_CONV // LANES):
        cols = slice(g * LANES, (g + 1) * LANES)
        zc = slice(OFF_C + g * LANES, OFF_C + (g + 1) * LANES)
        zv = slice(OFF_VA + g * LANES, OFF_VA + (g + 1) * LANES)
        zb = slice(OFF_B + g * LANES, OFF_B + (g + 1) * LANES)
        prev = (cp_ref[:, cols].astype(F32) * vp_ref[:, cols].astype(F32))[F32_ROWS:, :]
        nxt = (cn_ref[:, cols].astype(F32) * vn_ref[:, cols].astype(F32))[:F32_ROWS, :]
        cvs_ref[0:F32_ROWS, cols] = jnp.where(i > 0, prev, zero8)
        for r0 in range(0, tq, ROW_CHUNK):
            rows = slice(r0, r0 + ROW_CHUNK)
            cvs_ref[F32_ROWS + r0:F32_ROWS + r0 + ROW_CHUNK, cols] = (
                z_ref[rows, zc].astype(F32) * z_ref[rows, zv].astype(F32))
        cvs_ref[F32_ROWS + tq:2 * F32_ROWS + tq, cols] = jnp.where(i < nt - 1, nxt, zero8)
        for r0 in range(0, tq, ROW_CHUNK):
            lo = F32_ROWS + r0
            conv = (cvs_ref[lo - 1:lo - 1 + ROW_CHUNK, cols] * cw_ref[0:1, cols]
                    + cvs_ref[lo:lo + ROW_CHUNK, cols] * cw_ref[1:2, cols]
                    + cvs_ref[lo + 1:lo + 1 + ROW_CHUNK, cols] * cw_ref[2:3, cols])
            ya_s_ref[slot, r0:r0 + ROW_CHUNK, cols] = (
                z_ref[r0:r0 + ROW_CHUNK, zb].astype(F32) * conv).astype(BF16)


def _mixer(x, z, cos_t, sin_t, sink, conv_w, b_gate, w_out_a, w_o, w_mix, cast_weights):
    s, d = x.shape
    tq = MIX_TQ
    nt = s // tq
    kvb = tq // BLOCK
    cb = tq // BF16_ROWS
    qbcv = 4 * D_CONV
    kv_w = 2 * D_KV
    n_col = d // COL_CHUNK
    assert OFF_B == 0 and OFF_Q + D_ATTN == qbcv and OFF_K % kv_w == 0 and OFF_V == OFF_K + D_KV
    assert OFF_GA % COL_CHUNK == 0 and OFF_GB % COL_CHUNK == 0

    def br(step):
        return jnp.minimum(step, nt - 1)

    def mg(step):
        return jnp.maximum(step - 1, 0)

    def prev_blk(step, per):
        return jnp.maximum(br(step) * per - 1, 0)

    def next_blk(step, per, total):
        return jnp.minimum((br(step) + 1) * per, total - 1)

    def gate_spec(off, c):
        return pl.BlockSpec((tq, COL_CHUNK), lambda i: (mg(i), off // COL_CHUNK + c))

    const = lambda i: (0, 0)
    single = dict(pipeline_mode=pl.Buffered(1))
    in_specs = [
        pl.BlockSpec(memory_space=pltpu.SMEM),
        pl.BlockSpec((tq, qbcv), lambda i: (br(i), 0)),
        pl.BlockSpec((tq, kv_w), lambda i: (br(i), OFF_K // kv_w)),
        pl.BlockSpec((BLOCK, kv_w), lambda i: (prev_blk(i, kvb), OFF_K // kv_w)),
        pl.BlockSpec((BLOCK, kv_w), lambda i: (next_blk(i, kvb, s // BLOCK), OFF_K // kv_w)),
        pl.BlockSpec((BF16_ROWS, D_CONV), lambda i: (prev_blk(i, cb), OFF_C // D_CONV)),
        pl.BlockSpec((BF16_ROWS, D_CONV), lambda i: (prev_blk(i, cb), OFF_VA // D_CONV)),
        pl.BlockSpec((BF16_ROWS, D_CONV), lambda i: (next_blk(i, cb, s // BF16_ROWS), OFF_C // D_CONV)),
        pl.BlockSpec((BF16_ROWS, D_CONV), lambda i: (next_blk(i, cb, s // BF16_ROWS), OFF_VA // D_CONV)),
        pl.BlockSpec((tq, LANES), lambda i: (br(i), 0)),
        pl.BlockSpec((tq, LANES), lambda i: (br(i), 0)),
        pl.BlockSpec((BLOCK, LANES), lambda i: (prev_blk(i, kvb), 0)),
        pl.BlockSpec((BLOCK, LANES), lambda i: (prev_blk(i, kvb), 0)),
        pl.BlockSpec((BLOCK, LANES), lambda i: (next_blk(i, kvb, s // BLOCK), 0)),
        pl.BlockSpec((BLOCK, LANES), lambda i: (next_blk(i, kvb, s // BLOCK), 0)),
        pl.BlockSpec((3, D_CONV), const),
        pl.BlockSpec((tq, d), lambda i: (mg(i), 0)),
    ] + [gate_spec(OFF_GA, c) for c in range(n_col)] + [
        gate_spec(OFF_GB, c) for c in range(n_col)] + [
        pl.BlockSpec((1, 2 * d), const),
        pl.BlockSpec((D_CONV, d), const, **single),
        pl.BlockSpec((D_ATTN, d), const, **single),
        pl.BlockSpec((d, d), const, **single),
    ]
    r_ext = tq + 2 * BLOCK
    grp = N_HEADS // N_KV_HEADS
    slabs = [_slab_spec(cw.shape, nt, br) for cw in cast_weights]
    outs = pl.pallas_call(
        functools.partial(_mixer_kernel, seq_len=s, n_cast=len(cast_weights)),
        grid=(nt + 1,),
        in_specs=in_specs + slabs,
        out_specs=[pl.BlockSpec((tq, d), lambda i: (mg(i), 0))] + slabs,
        out_shape=[jax.ShapeDtypeStruct((s, d), F32)]
        + [jax.ShapeDtypeStruct(cw.shape, BF16) for cw in cast_weights],
        scratch_shapes=[
            pltpu.VMEM((2, tq, D_ATTN), BF16),
            pltpu.VMEM((2, tq, D_CONV), BF16),
            pltpu.VMEM((N_KV_HEADS, tq // BLOCK, grp * BLOCK, LANES), BF16),
            pltpu.VMEM((N_KV_HEADS, r_ext, LANES), BF16),
            pltpu.VMEM((N_KV_HEADS, HEAD_DIM + BF16_ROWS, r_ext), BF16),
            pltpu.VMEM((tq + 2 * F32_ROWS, D_CONV), F32),
            pltpu.VMEM((2, tq, COL_CHUNK), F32),
            pltpu.VMEM((2, tq, COL_CHUNK), F32),
            pltpu.VMEM((tq, d), BF16),
        ],
        compiler_params=pltpu.CompilerParams(
            dimension_semantics=("arbitrary",),
            vmem_limit_bytes=MIX_VMEM_LIMIT_BYTES),
        name="mixer",
    )(sink, z, z, z, z, z, z, z, z, cos_t, sin_t, cos_t, sin_t, cos_t, sin_t, conv_w, x,
      *([z] * (2 * n_col)), b_gate, w_out_a, w_o, w_mix, *cast_weights)
    return outs[0], outs[1:]


def _ffn_kernel(h_ref, hp_ref, hn_ref, g2_ref, wa_ref, wg_ref, cwa_ref, cwg_ref,
                ba_ref, bgt_ref, wd_ref, gf_ref,
                o_ref,
                u_ref, *sub_refs, final_norm):
    i = pl.program_id(0)
    j = pl.program_id(1)
    nt = pl.num_programs(0)
    nf = pl.num_programs(1)
    tm = h_ref.shape[0]
    halo = BF16_ROWS
    n_sub = len(sub_refs) // 3
    sa_refs, sg_refs, act_refs = sub_refs[0::3], sub_refs[1::3], sub_refs[2::3]

    @pl.when(j == 0)
    def _():
        g2 = g2_ref[...]
        zero = jnp.zeros((halo, h_ref.shape[1]), F32)
        u_ref[0:halo, :] = jnp.where(i > 0, _rmsnorm(hp_ref[...], g2), zero).astype(BF16)
        u_ref[halo + tm:2 * halo + tm, :] = jnp.where(
            i < nt - 1, _rmsnorm(hn_ref[...], g2), zero).astype(BF16)

        def body(r, carry):
            r0 = pl.multiple_of(r * ROW_CHUNK, ROW_CHUNK)
            u_ref[pl.ds(halo + r0, ROW_CHUNK), :] = _rmsnorm(
                h_ref[pl.ds(r0, ROW_CHUNK), :], g2).astype(BF16)
            o_ref[pl.ds(r0, ROW_CHUNK), :] = jnp.zeros((ROW_CHUNK, o_ref.shape[1]), F32)
            return carry
        lax.fori_loop(0, tm // ROW_CHUNK, body, 0)

    def conv(s_ref, cw_ref, b_ref, r0, cs):
        return (s_ref[r0 + halo - 1:r0 + halo - 1 + FFN_ACT_ROWS, :] * cw_ref[0:1, cs]
                + s_ref[r0 + halo:r0 + halo + FFN_ACT_ROWS, :] * cw_ref[1:2, cs]
                + s_ref[r0 + halo + 1:r0 + halo + 1 + FFN_ACT_ROWS, :] * cw_ref[2:3, cs]) + b_ref[:, cs]

    def sub_cols(k):
        return slice(k * FFN_SUB, (k + 1) * FFN_SUB)

    def up(k, w_ref, s_refs):
        s_refs[k][...] = jnp.dot(u_ref[...], w_ref[:, sub_cols(k)], preferred_element_type=F32)

    def activate(k):
        for r0 in range(0, tm, FFN_ACT_ROWS):
            a = conv(sa_refs[k], cwa_ref, ba_ref, r0, sub_cols(k))
            g = conv(sg_refs[k], cwg_ref, bgt_ref, r0, sub_cols(k))
            act_refs[k][r0:r0 + FFN_ACT_ROWS, :] = (a * jax.nn.sigmoid(a) * g).astype(BF16)

    def down(k):
        for c in range(o_ref.shape[1] // COL_CHUNK):
            cols = slice(c * COL_CHUNK, (c + 1) * COL_CHUNK)
            o_ref[:, cols] += jnp.dot(act_refs[k][...], wd_ref[sub_cols(k), cols],
                                      preferred_element_type=F32)

    for k in range(n_sub):
        up(k, wa_ref, sa_refs)
        up(k, wg_ref, sg_refs)
    for k in range(n_sub):
        activate(k)
        down(k)

    @pl.when(j == nf - 1)
    def _():
        def body(r, carry):
            rows = pl.ds(pl.multiple_of(r * ROW_CHUNK, ROW_CHUNK), ROW_CHUNK)
            h = h_ref[rows, :] + o_ref[rows, :]
            if final_norm:
                h = _rmsnorm(h, gf_ref[...])
            o_ref[rows, :] = h
            return carry
        lax.fori_loop(0, tm // ROW_CHUNK, body, 0)


def _ffn(h, g2, w_up, conv_w, conv_b, w_down, gf, final_norm):
    s, d = h.shape
    tm, tf = FFN_TM, FFN_TF
    nt, nf = s // tm, D_FF // tf
    hb = tm // BF16_ROWS
    in_specs = [
        pl.BlockSpec((tm, d), lambda i, j: (i, 0)),
        pl.BlockSpec((BF16_ROWS, d), lambda i, j: (jnp.maximum(i * hb - 1, 0), 0)),
        pl.BlockSpec((BF16_ROWS, d), lambda i, j: (jnp.minimum((i + 1) * hb, s // BF16_ROWS - 1), 0)),
        pl.BlockSpec((1, d), lambda i, j: (0, 0)),
        pl.BlockSpec((d, tf), lambda i, j: (0, j)),
        pl.BlockSpec((d, tf), lambda i, j: (0, nf + j)),
        pl.BlockSpec((3, tf), lambda i, j: (0, j)),
        pl.BlockSpec((3, tf), lambda i, j: (0, nf + j)),
        pl.BlockSpec((1, tf), lambda i, j: (0, j)),
        pl.BlockSpec((1, tf), lambda i, j: (0, nf + j)),
        pl.BlockSpec((tf, d), lambda i, j: (j, 0)),
        pl.BlockSpec((1, d), lambda i, j: (0, 0)),
    ]
    return pl.pallas_call(
        functools.partial(_ffn_kernel, final_norm=final_norm),
        grid=(nt, nf),
        in_specs=in_specs,
        out_specs=pl.BlockSpec((tm, d), lambda i, j: (i, 0)),
        out_shape=jax.ShapeDtypeStruct((s, d), F32),
        scratch_shapes=[
            pltpu.VMEM((tm + 2 * BF16_ROWS, d), BF16),
        ] + [
            pltpu.VMEM((tm + 2 * BF16_ROWS, FFN_SUB), F32),
            pltpu.VMEM((tm + 2 * BF16_ROWS, FFN_SUB), F32),
            pltpu.VMEM((tm, FFN_SUB), BF16),
        ] * (tf // FFN_SUB),
        compiler_params=pltpu.CompilerParams(
            dimension_semantics=("arbitrary", "arbitrary"),
            vmem_limit_bytes=VMEM_LIMIT_BYTES),
        name="ffn",
    )(h, h, h, g2, w_up, w_up, conv_w, conv_w, conv_b, conv_b, w_down, gf)


def _rope_tables(seq_len):
    half = HEAD_DIM // 2
    reps = LANES // half
    inv_freq = ROPE_THETA ** (-jnp.arange(0, half, dtype=F32) / half)
    pos = (reps * jnp.arange(seq_len // reps)[:, None]
           + (jnp.arange(LANES) // half)[None, :]).astype(F32)
    ang = pos * jnp.tile(inv_freq, reps)[None, :]
    cos = jnp.cos(ang).reshape(seq_len, half)
    sin = jnp.sin(ang).reshape(seq_len, half)
    cos_t = jnp.tile(cos, (1, reps))
    sin_t = jnp.tile(jnp.concatenate([-sin, sin], axis=1), (1, reps // 2))
    return cos_t, sin_t


def kernel(x, norm_mix_g, w_in, b_gate, conv_a_w, w_out_a, sink_logits, w_o_attn, w_mix_out,
           norm_ffn_g, ffn_w_up, ffn_conv_w, ffn_conv_b, ffn_w_down, norm_final_g):
    b, s, d = x.shape
    depth = w_in.shape[0]
    cos_t, sin_t = _rope_tables(s)
    outs = []
    for bi in range(b):
        h = x[bi]
        for l in range(depth):
            z, (w_oa, w_o, w_mix) = _in_proj(
                h, norm_mix_g[l][None, :], w_in[l].astype(BF16),
                (w_out_a[l], w_o_attn[l], w_mix_out[l]))
            h, (w_up, w_down) = _mixer(
                h, z, cos_t, sin_t, sink_logits[l], conv_a_w[l], b_gate[l][None, :],
                w_oa, w_o, w_mix, (ffn_w_up[l], ffn_w_down[l]))
            h = _ffn(h, norm_ffn_g[l][None, :], w_up, ffn_conv_w[l], ffn_conv_b[l][None, :],
                     w_down, norm_final_g[None, :], final_norm=(l == depth - 1))
        outs.append(h[None])
    return outs[0] if b == 1 else jnp.concatenate(outs, axis=0)
```

```python
import functools

import jax
import jax.numpy as jnp
from jax import lax
from jax.experimental import pallas as pl
from jax.experimental.pallas import tpu as pltpu

D_MODEL = 2048
D_CONV = D_MODEL // 2
N_HEADS = 16
N_KV_HEADS = 4
HEAD_DIM = 64
D_ATTN = N_HEADS * HEAD_DIM
D_KV = N_KV_HEADS * HEAD_DIM
WINDOW = 128
BLOCK = 128
ROPE_THETA = 10000.0
D_FF = 5632
EPS = 1e-6

OFF_B = 0
OFF_C = OFF_B + D_CONV
OFF_VA = OFF_C + D_CONV
OFF_Q = OFF_VA + D_CONV
OFF_K = OFF_Q + D_ATTN
OFF_V = OFF_K + D_KV
OFF_GA = OFF_V + D_KV
OFF_GB = OFF_GA + D_MODEL
D_IN_PROJ = OFF_GB + D_MODEL
LOG2E = 1.4426950408889634

LANES = 128
BF16_ROWS = 16
F32_ROWS = 8
VMEM_LIMIT_BYTES = 62 * 1024 * 1024
IN_VMEM_LIMIT_BYTES = 52 * 1024 * 1024
MIX_VMEM_LIMIT_BYTES = 52 * 1024 * 1024

IN_TM = 1024
IN_TN = 2176
MIX_TQ = 256
MIX_OUT_COLS = 256
MIX_LEAD_UNITS = 0
FFN_TM = 1024
FFN_TF = 512
FFN_SUB = 256
FFN_ACT_ROWS = 128
ROW_CHUNK = 128
COL_CHUNK = 512

BF16 = jnp.bfloat16
F32 = jnp.float32


def _rmsnorm(x, g):
    ms = jnp.mean(x * x, axis=-1, keepdims=True)
    return x * lax.rsqrt(ms + EPS) * g


def _in_proj_kernel(x_ref, g_ref, w_ref, *rest, n_cast):
    src_refs = rest[:n_cast]
    z_ref = rest[n_cast]
    dst_refs = rest[n_cast + 1:2 * n_cast + 1]
    u_ref = rest[2 * n_cast + 1]

    @pl.when(pl.program_id(1) == 0)
    def _():
        def body(r, carry):
            rows = pl.ds(pl.multiple_of(r * ROW_CHUNK, ROW_CHUNK), ROW_CHUNK)
            u_ref[rows, :] = _rmsnorm(x_ref[rows, :], g_ref[...]).astype(BF16)
            return carry
        lax.fori_loop(0, x_ref.shape[0] // ROW_CHUNK, body, 0)

    for src, dst in zip(src_refs, dst_refs):
        dst[...] = src[...].astype(dst.dtype)

    z_ref[...] = jnp.dot(u_ref[...], w_ref[...],
                         preferred_element_type=F32).astype(z_ref.dtype)


def _slab_spec(shape, n_steps, step_fn):
    rows, cols = shape
    per = rows // n_steps
    hold = 1
    while per % BF16_ROWS:
        per, hold = per * 2, hold * 2
    assert per * (n_steps // hold) == rows
    return pl.BlockSpec((per, cols), lambda *idx: (step_fn(*idx) // hold, 0))


def _in_proj(x, g, w, cast_weights):
    s, d = x.shape
    n = w.shape[1]
    grid = (s // IN_TM, n // IN_TN)
    slabs = [_slab_spec(cw.shape, grid[0] * grid[1], lambda i, j: i * grid[1] + j)
             for cw in cast_weights]
    outs = pl.pallas_call(
        functools.partial(_in_proj_kernel, n_cast=len(cast_weights)),
        grid=grid,
        in_specs=[
            pl.BlockSpec((IN_TM, d), lambda i, j: (i, 0)),
            pl.BlockSpec((1, d), lambda i, j: (0, 0)),
            pl.BlockSpec((d, IN_TN), lambda i, j: (0, j)),
        ] + slabs,
        out_specs=[pl.BlockSpec((IN_TM, IN_TN), lambda i, j: (i, j))] + slabs,
        out_shape=[jax.ShapeDtypeStruct((s, n), BF16)]
        + [jax.ShapeDtypeStruct(cw.shape, BF16) for cw in cast_weights],
        scratch_shapes=[pltpu.VMEM((IN_TM, d), BF16)],
        compiler_params=pltpu.CompilerParams(
            dimension_semantics=("arbitrary", "arbitrary"),
            vmem_limit_bytes=IN_VMEM_LIMIT_BYTES),
        name="in_proj",
    )(x, g, w, *cast_weights)
    return outs[0], outs[1:]


def _rope(x, cos, sin_signed, first_half):
    partner = jnp.where(first_half, pltpu.roll(x, LANES - HEAD_DIM // 2, 1),
                        pltpu.roll(x, HEAD_DIM // 2, 1))
    return x * cos + partner * sin_signed


def _mixer_kernel(sink_ref, z_ref, kv_ref, kvp_ref, kvn_ref, cp_ref, vp_ref, cn_ref, vn_ref,
                  cos_ref, sin_ref, cosp_ref, sinp_ref, cosn_ref, sinn_ref, cw_ref, x_ref,
                  *rest, seq_len, n_cast):
    n_col = x_ref.shape[1] // COL_CHUNK
    gla_refs, glb_refs = rest[:n_col], rest[n_col:2 * n_col]
    bg_ref, woa_ref, wo_ref, wmix_ref = rest[2 * n_col:2 * n_col + 4]
    rest = rest[2 * n_col + 4:]
    src_refs = rest[:n_cast]
    h1_ref = rest[n_cast]
    dst_refs = rest[n_cast + 1:2 * n_cast + 1]
    (att_s_ref, ya_s_ref, qst_ref, kz_ref, vt_ref, cvs_ref,
     ys_a_ref, ys_b_ref, mix_ref) = rest[2 * n_cast + 1:]
    for src, dst in zip(src_refs, dst_refs):
        dst[...] = src[...].astype(dst.dtype)

    step = pl.program_id(0)
    nt = pl.num_programs(0) - 1
    i = jnp.minimum(step, nt - 1)
    slot = step % 2
    pslot = 1 - slot
    tq = z_ref.shape[0]
    d = x_ref.shape[1]

    @pl.when(step == 0)
    def _():
        att_s_ref[1] = jnp.zeros(att_s_ref.shape[1:], BF16)
        ya_s_ref[1] = jnp.zeros(ya_s_ref.shape[1:], BF16)

    def merge_stage1(c):
        cols = slice(c * COL_CHUNK, (c + 1) * COL_CHUNK)
        gcols = slice(d + c * COL_CHUNK, d + (c + 1) * COL_CHUNK)
        ys_a_ref[c % 2] = jnp.dot(ya_s_ref[pslot], woa_ref[:, cols], preferred_element_type=F32)
        ys_b_ref[c % 2] = jnp.dot(att_s_ref[pslot], wo_ref[:, cols], preferred_element_type=F32)
        for r0 in range(0, tq, ROW_CHUNK):
            rows = slice(r0, r0 + ROW_CHUNK)
            g_a = jax.nn.sigmoid(gla_refs[c][rows, :].astype(F32) + bg_ref[:, cols])
            g_b = jax.nn.sigmoid(glb_refs[c][rows, :].astype(F32) + bg_ref[:, gcols])
            mix_ref[rows, cols] = (g_a * ys_a_ref[c % 2, rows, :]
                                   + g_b * ys_b_ref[c % 2, rows, :]).astype(BF16)

    def merge_stage2(c):
        cols = slice(c * MIX_OUT_COLS, (c + 1) * MIX_OUT_COLS)
        h1_ref[:, cols] = x_ref[:, cols] + jnp.dot(mix_ref[...], wmix_ref[:, cols],
                                                   preferred_element_type=F32)

    for c in range(n_col):
        merge_stage1(c)
    merge_pieces = [functools.partial(merge_stage2, c) for c in range(d // MIX_OUT_COLS)]

    nb = tq // BLOCK
    grp = N_HEADS // N_KV_HEADS
    lane = lax.broadcasted_iota(jnp.int32, (1, LANES), 1)
    first_half = (lane % HEAD_DIM) < (HEAD_DIM // 2)
    low_head = lane < HEAD_DIM
    scale = HEAD_DIM ** -0.5 * LOG2E

    for b in range(nb):
        rows = slice(b * BLOCK, (b + 1) * BLOCK)
        cos_b = cos_ref[rows, :]
        sin_b = sin_ref[rows, :]
        for g in range(D_ATTN // LANES):
            h = g // 2
            xq = z_ref[rows, OFF_Q + g * LANES:OFF_Q + (g + 1) * LANES].astype(F32)
            xr = _rope(xq, cos_b, sin_b, first_half) * scale
            rolled = pltpu.roll(xr, HEAD_DIM, 1)
            even, odd = (xr, rolled) if h % 2 == 0 else (rolled, xr)
            j0 = 2 * (g % 2)
            qst_ref[h, b, j0 * BLOCK:(j0 + 1) * BLOCK, :] = even.astype(BF16)
            qst_ref[h, b, (j0 + 1) * BLOCK:(j0 + 2) * BLOCK, :] = odd.astype(BF16)

    sub = lax.broadcasted_iota(jnp.int32, (BF16_ROWS, BLOCK), 0)
    ones_rows = jnp.where(sub == 0, 1.0, 0.0).astype(F32)

    def put_kv(kv_blk_ref, c_ref, s_ref, blk, row0):
        src = slice(blk * BLOCK, (blk + 1) * BLOCK)
        dst = slice(row0, row0 + BLOCK)
        c = c_ref[src, :]
        s = s_ref[src, :]
        for pr in range(N_KV_HEADS // 2):
            kf = _rope(kv_blk_ref[src, pr * LANES:(pr + 1) * LANES].astype(F32), c, s, first_half)
            zero = jnp.zeros_like(kf)
            kz_ref[2 * pr, dst, :] = jnp.where(low_head, kf, zero).astype(BF16)
            kz_ref[2 * pr + 1, dst, :] = jnp.where(low_head, zero, kf).astype(BF16)
            vb_t = kv_blk_ref[src, D_KV + pr * LANES:D_KV + (pr + 1) * LANES].astype(F32).T
            vt_ref[2 * pr, :, dst] = jnp.concatenate(
                [vb_t[:HEAD_DIM], ones_rows], axis=0).astype(BF16)
            vt_ref[2 * pr + 1, :, dst] = jnp.concatenate(
                [vb_t[HEAD_DIM:], ones_rows], axis=0).astype(BF16)

    put_kv(kvp_ref, cosp_ref, sinp_ref, 0, 0)
    for blk in range(nb):
        put_kv(kv_ref, cos_ref, sin_ref, blk, (blk + 1) * BLOCK)
    put_kv(kvn_ref, cosn_ref, sinn_ref, 0, BLOCK + tq)

    c_io = lax.broadcasted_iota(jnp.int32, (BLOCK, BLOCK), 0)
    r_io = lax.broadcasted_iota(jnp.int32, (BLOCK, BLOCK), 1)

    def scores(b, h):
        return lax.dot_general(kz_ref[h, b * BLOCK:(b + 3) * BLOCK, :], qst_ref[h, b],
                               (((1,), (1,)), ((), ())),
                               preferred_element_type=F32)

    def finish(b, h, s_t):
        base = i * tq + b * BLOCK
        mask_lo = (c_io >= r_io) & (c_io + (base - BLOCK) >= 0)
        mask_hi = (c_io <= r_io) & (c_io + (base + BLOCK) < seq_len)
        probs, sink_terms = [], []
        for j in range(grp):
            sink = sink_ref[grp * h + j] * LOG2E
            cols = slice(j * BLOCK, (j + 1) * BLOCK)
            lo = jnp.where(mask_lo, s_t[0:BLOCK, cols], -jnp.inf)
            mid = s_t[BLOCK:2 * BLOCK, cols]
            hi = jnp.where(mask_hi, s_t[2 * BLOCK:3 * BLOCK, cols], -jnp.inf)
            m = jnp.maximum(jnp.maximum(jnp.max(lo, axis=0, keepdims=True),
                                        jnp.max(hi, axis=0, keepdims=True)),
                            jnp.maximum(jnp.max(mid, axis=0, keepdims=True), sink))
            probs.append(jnp.concatenate(
                [jnp.exp2(lo - m), jnp.exp2(mid - m), jnp.exp2(hi - m)], axis=0).astype(BF16))
            sink_terms.append(jnp.exp2(sink - m))
        p_t = jnp.concatenate(probs, axis=1)
        o_t = jnp.dot(vt_ref[h, :, b * BLOCK:(b + 3) * BLOCK], p_t,
                      preferred_element_type=F32)
        outs = []
        for j in range(grp):
            cols = slice(j * BLOCK, (j + 1) * BLOCK)
            denom = o_t[HEAD_DIM:HEAD_DIM + 1, cols] + sink_terms[j]
            outs.append(o_t[:HEAD_DIM, cols] * (1.0 / denom))
        for pr in range(grp // 2):
            pair_t = jnp.concatenate([outs[2 * pr], outs[2 * pr + 1]], axis=0)
            g = (grp // 2) * h + pr
            att_s_ref[slot, b * BLOCK:(b + 1) * BLOCK, g * LANES:(g + 1) * LANES] = (
                pair_t.T.astype(BF16))

    units = [(b, h) for b in range(nb) for h in range(N_KV_HEADS)]
    n_units, n_pieces = len(units), len(merge_pieces)
    lead = MIX_LEAD_UNITS
    s_next = scores(*units[0])
    for u, (b, h) in enumerate(units):
        s_t = s_next
        if u + 1 < n_units:
            s_next = scores(*units[u + 1])
        if u >= lead:
            for piece in merge_pieces[(u - lead) * n_pieces // (n_units - lead):
                                      (u - lead + 1) * n_pieces // (n_units - lead)]:
                piece()
        finish(b, h, s_t)

    zero8 = jnp.zeros((F32_ROWS, LANES), F32)
    for g in range(D_CONV // LANES):
        cols = slice(g * LANES, (g + 1) * LANES)
        zc = slice(OFF_C + g * LANES, OFF_C + (g + 1) * LANES)
        zv = slice(OFF_VA + g * LANES, OFF_VA + (g + 1) * LANES)
        zb = slice(OFF_B + g * LANES, OFF_B + (g + 1) * LANES)
        prev = (cp_ref[:, cols].astype(F32) * vp_ref[:, cols].astype(F32))[F32_ROWS:, :]
        nxt = (cn_ref[:, cols].astype(F32) * vn_ref[:, cols].astype(F32))[:F32_ROWS, :]
        cvs_ref[0:F32_ROWS, cols] = jnp.where(i > 0, prev, zero8)
        for r0 in range(0, tq, ROW_CHUNK):
            rows = slice(r0, r0 + ROW_CHUNK)
            cvs_ref[F32_ROWS + r0:F32_ROWS + r0 + ROW_CHUNK, cols] = (
                z_ref[rows, zc].astype(F32) * z_ref[rows, zv].astype(F32))
        cvs_ref[F32_ROWS + tq:2 * F32_ROWS + tq, cols] = jnp.where(i < nt - 1, nxt, zero8)
        for r0 in range(0, tq, ROW_CHUNK):
            lo = F32_ROWS + r0
            conv = (cvs_ref[lo - 1:lo - 1 + ROW_CHUNK, cols] * cw_ref[0:1, cols]
                    + cvs_ref[lo:lo + ROW_CHUNK, cols] * cw_ref[1:2, cols]
                    + cvs_ref[lo + 1:lo + 1 + ROW_CHUNK, cols] * cw_ref[2:3, cols])
            ya_s_ref[slot, r0:r0 + ROW_CHUNK, cols] = (
                z_ref[r0:r0 + ROW_CHUNK, zb].astype(F32) * conv).astype(BF16)


def _mixer(x, z, cos_t, sin_t, sink, conv_w, b_gate, w_out_a, w_o, w_mix, cast_weights):
    s, d = x.shape
    tq = MIX_TQ
    nt = s // tq
    kvb = tq // BLOCK
    cb = tq // BF16_ROWS
    qbcv = 4 * D_CONV
    kv_w = 2 * D_KV
    n_col = d // COL_CHUNK
    assert OFF_B == 0 and OFF_Q + D_ATTN == qbcv and OFF_K % kv_w == 0 and OFF_V == OFF_K + D_KV
    assert OFF_GA % COL_CHUNK == 0 and OFF_GB % COL_CHUNK == 0

    def br(step):
        return jnp.minimum(step, nt - 1)

    def mg(step):
        return jnp.maximum(step - 1, 0)

    def prev_blk(step, per):
        return jnp.maximum(br(step) * per - 1, 0)

    def next_blk(step, per, total):
        return jnp.minimum((br(step) + 1) * per, total - 1)

    def gate_spec(off, c):
        return pl.BlockSpec((tq, COL_CHUNK), lambda i: (mg(i), off // COL_CHUNK + c))

    const = lambda i: (0, 0)
    single = dict(pipeline_mode=pl.Buffered(1))
    in_specs = [
        pl.BlockSpec(memory_space=pltpu.SMEM),
        pl.BlockSpec((tq, qbcv), lambda i: (br(i), 0)),
        pl.BlockSpec((tq, kv_w), lambda i: (br(i), OFF_K // kv_w)),
        pl.BlockSpec((BLOCK, kv_w), lambda i: (prev_blk(i, kvb), OFF_K // kv_w)),
        pl.BlockSpec((BLOCK, kv_w), lambda i: (next_blk(i, kvb, s // BLOCK), OFF_K // kv_w)),
        pl.BlockSpec((BF16_ROWS, D_CONV), lambda i: (prev_blk(i, cb), OFF_C // D_CONV)),
        pl.BlockSpec((BF16_ROWS, D_CONV), lambda i: (prev_blk(i, cb), OFF_VA // D_CONV)),
        pl.BlockSpec((BF16_ROWS, D_CONV), lambda i: (next_blk(i, cb, s // BF16_ROWS), OFF_C // D_CONV)),
        pl.BlockSpec((BF16_ROWS, D_CONV), lambda i: (next_blk(i, cb, s // BF16_ROWS), OFF_VA // D_CONV)),
        pl.BlockSpec((tq, LANES), lambda i: (br(i), 0)),
        pl.BlockSpec((tq, LANES), lambda i: (br(i), 0)),
        pl.BlockSpec((BLOCK, LANES), lambda i: (prev_blk(i, kvb), 0)),
        pl.BlockSpec((BLOCK, LANES), lambda i: (prev_blk(i, kvb), 0)),
        pl.BlockSpec((BLOCK, LANES), lambda i: (next_blk(i, kvb, s // BLOCK), 0)),
        pl.BlockSpec((BLOCK, LANES), lambda i: (next_blk(i, kvb, s // BLOCK), 0)),
        pl.BlockSpec((3, D_CONV), const),
        pl.BlockSpec((tq, d), lambda i: (mg(i), 0)),
    ] + [gate_spec(OFF_GA, c) for c in range(n_col)] + [
        gate_spec(OFF_GB, c) for c in range(n_col)] + [
        pl.BlockSpec((1, 2 * d), const),
        pl.BlockSpec((D_CONV, d), const, **single),
        pl.BlockSpec((D_ATTN, d), const, **single),
        pl.BlockSpec((d, d), const, **single),
    ]
    r_ext = tq + 2 * BLOCK
    grp = N_HEADS // N_KV_HEADS
    slabs = [_slab_spec(cw.shape, nt, br) for cw in cast_weights]
    outs = pl.pallas_call(
        functools.partial(_mixer_kernel, seq_len=s, n_cast=len(cast_weights)),
        grid=(nt + 1,),
        in_specs=in_specs + slabs,
        out_specs=[pl.BlockSpec((tq, d), lambda i: (mg(i), 0))] + slabs,
        out_shape=[jax.ShapeDtypeStruct((s, d), F32)]
        + [jax.ShapeDtypeStruct(cw.shape, BF16) for cw in cast_weights],
        scratch_shapes=[
            pltpu.VMEM((2, tq, D_ATTN), BF16),
            pltpu.VMEM((2, tq, D_CONV), BF16),
            pltpu.VMEM((N_KV_HEADS, tq // BLOCK, grp * BLOCK, LANES), BF16),
            pltpu.VMEM((N_KV_HEADS, r_ext, LANES), BF16),
            pltpu.VMEM((N_KV_HEADS, HEAD_DIM + BF16_ROWS, r_ext), BF16),
            pltpu.VMEM((tq + 2 * F32_ROWS, D_CONV), F32),
            pltpu.VMEM((2, tq, COL_CHUNK), F32),
            pltpu.VMEM((2, tq, COL_CHUNK), F32),
            pltpu.VMEM((tq, d), BF16),
        ],
        compiler_params=pltpu.CompilerParams(
            dimension_semantics=("arbitrary",),
            vmem_limit_bytes=MIX_VMEM_LIMIT_BYTES),
        name="mixer",
    )(sink, z, z, z, z, z, z, z, z, cos_t, sin_t, cos_t, sin_t, cos_t, sin_t, conv_w, x,
      *([z] * (2 * n_col)), b_gate, w_out_a, w_o, w_mix, *cast_weights)
    return outs[0], outs[1:]


def _ffn_kernel(h_ref, hp_ref, hn_ref, g2_ref, wa_ref, wg_ref, cwa_ref, cwg_ref,
                ba_ref, bgt_ref, wd_ref, gf_ref,
                o_ref,
                u_ref, act_ref, *sub_refs, final_norm):
    i = pl.program_id(0)
    j = pl.program_id(1)
    nt = pl.num_programs(0)
    nf = pl.num_programs(1) - 1
    tm = h_ref.shape[0]
    halo = BF16_ROWS
    n_sub = len(sub_refs) // 2
    sa_refs, sg_refs = sub_refs[0::2], sub_refs[1::2]
    slot = j % 2
    pslot = 1 - slot

    @pl.when(j == 0)
    def _():
        g2 = g2_ref[...]
        zero = jnp.zeros((halo, h_ref.shape[1]), F32)
        u_ref[0:halo, :] = jnp.where(i > 0, _rmsnorm(hp_ref[...], g2), zero).astype(BF16)
        u_ref[halo + tm:2 * halo + tm, :] = jnp.where(
            i < nt - 1, _rmsnorm(hn_ref[...], g2), zero).astype(BF16)

        def body(r, carry):
            r0 = pl.multiple_of(r * ROW_CHUNK, ROW_CHUNK)
            u_ref[pl.ds(halo + r0, ROW_CHUNK), :] = _rmsnorm(
                h_ref[pl.ds(r0, ROW_CHUNK), :], g2).astype(BF16)
            o_ref[pl.ds(r0, ROW_CHUNK), :] = jnp.zeros((ROW_CHUNK, o_ref.shape[1]), F32)
            return carry
        lax.fori_loop(0, tm // ROW_CHUNK, body, 0)

    def conv(s_ref, cw_ref, b_ref, r0, cs):
        return (s_ref[r0 + halo - 1:r0 + halo - 1 + FFN_ACT_ROWS, :] * cw_ref[0:1, cs]
                + s_ref[r0 + halo:r0 + halo + FFN_ACT_ROWS, :] * cw_ref[1:2, cs]
                + s_ref[r0 + halo + 1:r0 + halo + 1 + FFN_ACT_ROWS, :] * cw_ref[2:3, cs]) + b_ref[:, cs]

    def sub_cols(k):
        return slice(k * FFN_SUB, (k + 1) * FFN_SUB)

    def up(k, w_ref, s_refs):
        s_refs[k][...] = jnp.dot(u_ref[...], w_ref[:, sub_cols(k)], preferred_element_type=F32)

    def activate(k):
        for r0 in range(0, tm, FFN_ACT_ROWS):
            a = conv(sa_refs[k], cwa_ref, ba_ref, r0, sub_cols(k))
            g = conv(sg_refs[k], cwg_ref, bgt_ref, r0, sub_cols(k))
            act_ref[slot, r0:r0 + FFN_ACT_ROWS, sub_cols(k)] = (
                a * jax.nn.sigmoid(a) * g).astype(BF16)

    def down_previous():
        for c in range(o_ref.shape[1] // COL_CHUNK):
            cols = slice(c * COL_CHUNK, (c + 1) * COL_CHUNK)
            o_ref[:, cols] += jnp.dot(act_ref[pslot], wd_ref[:, cols], preferred_element_type=F32)

    def step_body(first, last):
        if not last:
            for k in range(n_sub):
                up(k, wa_ref, sa_refs)
                up(k, wg_ref, sg_refs)
        if not first:
            down_previous()
        if not last:
            for k in range(n_sub):
                activate(k)

    pl.when(j == 0)(functools.partial(step_body, True, False))
    pl.when((j > 0) & (j < nf))(functools.partial(step_body, False, False))
    pl.when(j == nf)(functools.partial(step_body, False, True))

    @pl.when(j == nf)
    def _():
        def body(r, carry):
            rows = pl.ds(pl.multiple_of(r * ROW_CHUNK, ROW_CHUNK), ROW_CHUNK)
            h = h_ref[rows, :] + o_ref[rows, :]
            if final_norm:
                h = _rmsnorm(h, gf_ref[...])
            o_ref[rows, :] = h
            return carry
        lax.fori_loop(0, tm // ROW_CHUNK, body, 0)


def _ffn(h, g2, w_up, conv_w, conv_b, w_down, gf, final_norm):
    s, d = h.shape
    tm, tf = FFN_TM, FFN_TF
    nt, nf = s // tm, D_FF // tf
    hb = tm // BF16_ROWS
    n_sub = tf // FFN_SUB

    def cur(j):
        return jnp.minimum(j, nf - 1)

    def prev(j):
        return jnp.maximum(j - 1, 0)

    in_specs = [
        pl.BlockSpec((tm, d), lambda i, j: (i, 0)),
        pl.BlockSpec((BF16_ROWS, d), lambda i, j: (jnp.maximum(i * hb - 1, 0), 0)),
        pl.BlockSpec((BF16_ROWS, d), lambda i, j: (jnp.minimum((i + 1) * hb, s // BF16_ROWS - 1), 0)),
        pl.BlockSpec((1, d), lambda i, j: (0, 0)),
        pl.BlockSpec((d, tf), lambda i, j: (0, cur(j))),
        pl.BlockSpec((d, tf), lambda i, j: (0, nf + cur(j))),
        pl.BlockSpec((3, tf), lambda i, j: (0, cur(j))),
        pl.BlockSpec((3, tf), lambda i, j: (0, nf + cur(j))),
        pl.BlockSpec((1, tf), lambda i, j: (0, cur(j))),
        pl.BlockSpec((1, tf), lambda i, j: (0, nf + cur(j))),
        pl.BlockSpec((tf, d), lambda i, j: (prev(j), 0)),
        pl.BlockSpec((1, d), lambda i, j: (0, 0)),
    ]
    return pl.pallas_call(
        functools.partial(_ffn_kernel, final_norm=final_norm),
        grid=(nt, nf + 1),
        in_specs=in_specs,
        out_specs=pl.BlockSpec((tm, d), lambda i, j: (i, 0)),
        out_shape=jax.ShapeDtypeStruct((s, d), F32),
        scratch_shapes=[
            pltpu.VMEM((tm + 2 * BF16_ROWS, d), BF16),
            pltpu.VMEM((2, tm, tf), BF16),
        ] + [
            pltpu.VMEM((tm + 2 * BF16_ROWS, FFN_SUB), F32),
            pltpu.VMEM((tm + 2 * BF16_ROWS, FFN_SUB), F32),
        ] * n_sub,
        compiler_params=pltpu.CompilerParams(
            dimension_semantics=("arbitrary", "arbitrary"),
            vmem_limit_bytes=VMEM_LIMIT_BYTES),
        name="ffn",
    )(h, h, h, g2, w_up, w_up, conv_w, conv_w, conv_b, conv_b, w_down, gf)


def _rope_tables(seq_len):
    half = HEAD_DIM // 2
    reps = LANES // half
    inv_freq = ROPE_THETA ** (-jnp.arange(0, half, dtype=F32) / half)
    pos = (reps * jnp.arange(seq_len // reps)[:, None]
           + (jnp.arange(LANES) // half)[None, :]).astype(F32)
    ang = pos * jnp.tile(inv_freq, reps)[None, :]
    cos = jnp.cos(ang).reshape(seq_len, half)
    sin = jnp.sin(ang).reshape(seq_len, half)
    cos_t = jnp.tile(cos, (1, reps))
    sin_t = jnp.tile(jnp.concatenate([-sin, sin], axis=1), (1, reps // 2))
    return cos_t, sin_t


def kernel(x, norm_mix_g, w_in, b_gate, conv_a_w, w_out_a, sink_logits, w_o_attn, w_mix_out,
           norm_ffn_g, ffn_w_up, ffn_conv_w, ffn_conv_b, ffn_w_down, norm_final_g):
    b, s, d = x.shape
    depth = w_in.shape[0]
    cos_t, sin_t = _rope_tables(s)
    outs = []
    for bi in range(b):
        h = x[bi]
        for l in range(depth):
            z, (w_oa, w_o, w_mix) = _in_proj(
                h, norm_mix_g[l][None, :], w_in[l].astype(BF16),
                (w_out_a[l], w_o_attn[l], w_mix_out[l]))
            h, (w_up, w_down) = _mixer(
                h, z, cos_t, sin_t, sink_logits[l], conv_a_w[l], b_gate[l][None, :],
                w_oa, w_o, w_mix, (ffn_w_up[l], ffn_w_down[l]))
            h = _ffn(h, norm_ffn_g[l][None, :], w_up, ffn_conv_w[l], ffn_conv_b[l][None, :],
                     w_down, norm_final_g[None, :], final_norm=(l == depth - 1))
        outs.append(h[None])
    return outs[0] if b == 1 else jnp.concatenate(outs, axis=0)
```

```python
import functools

import jax
import jax.numpy as jnp
from jax import lax
from jax.experimental import pallas as pl
from jax.experimental.pallas import tpu as pltpu

D_MODEL = 2048
D_CONV = D_MODEL // 2
N_HEADS = 16
N_KV_HEADS = 4
HEAD_DIM = 64
D_ATTN = N_HEADS * HEAD_DIM
D_KV = N_KV_HEADS * HEAD_DIM
WINDOW = 128
BLOCK = 128
ROPE_THETA = 10000.0
D_FF = 5632
EPS = 1e-6

OFF_B = 0
OFF_C = OFF_B + D_CONV
OFF_VA = OFF_C + D_CONV
OFF_Q = OFF_VA + D_CONV
OFF_K = OFF_Q + D_ATTN
OFF_V = OFF_K + D_KV
OFF_GA = OFF_V + D_KV
OFF_GB = OFF_GA + D_MODEL
D_IN_PROJ = OFF_GB + D_MODEL
LOG2E = 1.4426950408889634
ROPE_SPAN = 256

LANES = 128
BF16_ROWS = 16
F32_ROWS = 8
VMEM_LIMIT_BYTES = 60000 * 1024
IN_VMEM_LIMIT_BYTES = 52 * 1024 * 1024
MIX_VMEM_LIMIT_BYTES = 52 * 1024 * 1024

IN_TM = 1024
IN_TN = 2176
MIX_TQ = 256
MIX_OUT_COLS = 256
FFN_TM = 1024
FFN_TF = 512
FFN_SUB = 256
FFN_ACT_ROWS = 128
FFN_UP_SPLIT = 2
ROW_CHUNK = 128
COL_CHUNK = 512

BF16 = jnp.bfloat16
F32 = jnp.float32


def _rmsnorm(x, g):
    ms = jnp.mean(x * x, axis=-1, keepdims=True)
    return x * lax.rsqrt(ms + EPS) * g


def _in_proj_kernel(x_ref, g_ref, w_ref, *rest, n_cast):
    src_refs = rest[:n_cast]
    z_ref = rest[n_cast]
    dst_refs = rest[n_cast + 1:2 * n_cast + 1]
    u_ref = rest[2 * n_cast + 1]

    @pl.when(pl.program_id(1) == 0)
    def _():
        def body(r, carry):
            rows = pl.ds(pl.multiple_of(r * ROW_CHUNK, ROW_CHUNK), ROW_CHUNK)
            u_ref[rows, :] = _rmsnorm(x_ref[rows, :], g_ref[...]).astype(BF16)
            return carry
        lax.fori_loop(0, x_ref.shape[0] // ROW_CHUNK, body, 0)

    for src, dst in zip(src_refs, dst_refs):
        dst[...] = src[...].astype(dst.dtype)

    z_ref[...] = jnp.dot(u_ref[...], w_ref[...],
                         preferred_element_type=F32).astype(z_ref.dtype)


def _slab_spec(shape, n_steps, step_fn):
    rows, cols = shape
    per = rows // n_steps
    hold = 1
    while per % BF16_ROWS:
        per, hold = per * 2, hold * 2
    assert per * (n_steps // hold) == rows
    return pl.BlockSpec((per, cols), lambda *idx: (step_fn(*idx) // hold, 0))


def _in_proj(x, g, w, cast_weights):
    s, d = x.shape
    n = w.shape[1]
    grid = (s // IN_TM, n // IN_TN)
    slabs = [_slab_spec(cw.shape, grid[0] * grid[1], lambda i, j: i * grid[1] + j)
             for cw in cast_weights]
    outs = pl.pallas_call(
        functools.partial(_in_proj_kernel, n_cast=len(cast_weights)),
        grid=grid,
        in_specs=[
            pl.BlockSpec((IN_TM, d), lambda i, j: (i, 0)),
            pl.BlockSpec((1, d), lambda i, j: (0, 0)),
            pl.BlockSpec((d, IN_TN), lambda i, j: (0, j)),
        ] + slabs,
        out_specs=[pl.BlockSpec((IN_TM, IN_TN), lambda i, j: (i, j))] + slabs,
        out_shape=[jax.ShapeDtypeStruct((s, n), BF16)]
        + [jax.ShapeDtypeStruct(cw.shape, BF16) for cw in cast_weights],
        scratch_shapes=[pltpu.VMEM((IN_TM, d), BF16)],
        compiler_params=pltpu.CompilerParams(
            dimension_semantics=("arbitrary", "arbitrary"),
            vmem_limit_bytes=IN_VMEM_LIMIT_BYTES),
        name="in_proj",
    )(x, g, w, *cast_weights)
    return outs[0], outs[1:]


def _rope(x, cos, sin_signed, first_half):
    partner = jnp.where(first_half, pltpu.roll(x, LANES - HEAD_DIM // 2, 1),
                        pltpu.roll(x, HEAD_DIM // 2, 1))
    return x * cos + partner * sin_signed


def _mixer_kernel(sink_ref, z_ref, kv_ref, kvp_ref, kvn_ref, cp_ref, vp_ref, cn_ref, vn_ref,
                  cos_ref, sin_ref, cosp_ref, sinp_ref, cosn_ref, sinn_ref, cw_ref, x_ref,
                  *rest, seq_len, n_cast):
    n_col = x_ref.shape[1] // COL_CHUNK
    gla_refs, glb_refs = rest[:n_col], rest[n_col:2 * n_col]
    bg_ref, woa_ref, wo_ref, wmix_ref = rest[2 * n_col:2 * n_col + 4]
    rest = rest[2 * n_col + 4:]
    src_refs = rest[:n_cast]
    h1_ref = rest[n_cast]
    dst_refs = rest[n_cast + 1:2 * n_cast + 1]
    (att_s_ref, ya_s_ref, qst_ref, kz_ref, vt_ref, cvs_ref,
     ys_a_ref, ys_b_ref, mix_ref) = rest[2 * n_cast + 1:]
    for src, dst in zip(src_refs, dst_refs):
        dst[...] = src[...].astype(dst.dtype)

    step = pl.program_id(0)
    nt = pl.num_programs(0) - 1
    i = jnp.minimum(step, nt - 1)
    slot = step % 2
    pslot = 1 - slot
    tq = z_ref.shape[0]
    d = x_ref.shape[1]

    @pl.when(step == 0)
    def _():
        att_s_ref[1] = jnp.zeros(att_s_ref.shape[1:], BF16)
        ya_s_ref[1] = jnp.zeros(ya_s_ref.shape[1:], BF16)

    def merge_stage1(c):
        cols = slice(c * COL_CHUNK, (c + 1) * COL_CHUNK)
        gcols = slice(d + c * COL_CHUNK, d + (c + 1) * COL_CHUNK)
        ys_a_ref[c % 2] = jnp.dot(ya_s_ref[pslot], woa_ref[:, cols], preferred_element_type=F32)
        ys_b_ref[c % 2] = jnp.dot(att_s_ref[pslot], wo_ref[:, cols], preferred_element_type=F32)
        for r0 in range(0, tq, ROW_CHUNK):
            rows = slice(r0, r0 + ROW_CHUNK)
            g_a = jax.nn.sigmoid(gla_refs[c][rows, :].astype(F32) + bg_ref[:, cols])
            g_b = jax.nn.sigmoid(glb_refs[c][rows, :].astype(F32) + bg_ref[:, gcols])
            mix_ref[rows, cols] = (g_a * ys_a_ref[c % 2, rows, :]
                                   + g_b * ys_b_ref[c % 2, rows, :]).astype(BF16)

    def merge_stage2(c):
        cols = slice(c * MIX_OUT_COLS, (c + 1) * MIX_OUT_COLS)
        h1_ref[:, cols] = x_ref[:, cols] + jnp.dot(mix_ref[...], wmix_ref[:, cols],
                                                   preferred_element_type=F32)

    for c in range(n_col):
        merge_stage1(c)
    merge_pieces = [functools.partial(merge_stage2, c) for c in range(d // MIX_OUT_COLS)]

    nb = tq // BLOCK
    grp = N_HEADS // N_KV_HEADS
    lane = lax.broadcasted_iota(jnp.int32, (1, LANES), 1)
    first_half = (lane % HEAD_DIM) < (HEAD_DIM // 2)
    low_head = lane < HEAD_DIM
    scale = HEAD_DIM ** -0.5 * LOG2E

    for b in range(nb):
        rows = slice(b * BLOCK, (b + 1) * BLOCK)
        cos_b = cos_ref[rows, :]
        sin_b = sin_ref[rows, :]
        for g in range(D_ATTN // LANES):
            h = g // 2
            xq = z_ref[rows, OFF_Q + g * LANES:OFF_Q + (g + 1) * LANES].astype(F32)
            xr = _rope(xq, cos_b, sin_b, first_half) * scale
            rolled = pltpu.roll(xr, HEAD_DIM, 1)
            even, odd = (xr, rolled) if h % 2 == 0 else (rolled, xr)
            j0 = 2 * (g % 2)
            qst_ref[h, b, j0 * BLOCK:(j0 + 1) * BLOCK, :] = even.astype(BF16)
            qst_ref[h, b, (j0 + 1) * BLOCK:(j0 + 2) * BLOCK, :] = odd.astype(BF16)

    sub = lax.broadcasted_iota(jnp.int32, (BF16_ROWS, BLOCK), 0)
    ones_rows = jnp.where(sub == 0, 1.0, 0.0).astype(F32)

    def put_kv(kv_blk_ref, c_ref, s_ref, blk, row0):
        src = slice(blk * BLOCK, (blk + 1) * BLOCK)
        dst = slice(row0, row0 + BLOCK)
        c = c_ref[src, :]
        s = s_ref[src, :]
        for pr in range(N_KV_HEADS // 2):
            kf = _rope(kv_blk_ref[src, pr * LANES:(pr + 1) * LANES].astype(F32), c, s, first_half)
            zero = jnp.zeros_like(kf)
            kz_ref[2 * pr, dst, :] = jnp.where(low_head, kf, zero).astype(BF16)
            kz_ref[2 * pr + 1, dst, :] = jnp.where(low_head, zero, kf).astype(BF16)
            vb_t = kv_blk_ref[src, D_KV + pr * LANES:D_KV + (pr + 1) * LANES].astype(F32).T
            vt_ref[2 * pr, :, dst] = jnp.concatenate(
                [vb_t[:HEAD_DIM], ones_rows], axis=0).astype(BF16)
            vt_ref[2 * pr + 1, :, dst] = jnp.concatenate(
                [vb_t[HEAD_DIM:], ones_rows], axis=0).astype(BF16)

    put_kv(kvp_ref, cosp_ref, sinp_ref, 0, 0)
    for blk in range(nb):
        put_kv(kv_ref, cos_ref, sin_ref, blk, (blk + 1) * BLOCK)
    put_kv(kvn_ref, cosn_ref, sinn_ref, 0, BLOCK + tq)

    c_io = lax.broadcasted_iota(jnp.int32, (BLOCK, BLOCK), 0)
    r_io = lax.broadcasted_iota(jnp.int32, (BLOCK, BLOCK), 1)

    def scores(b, h):
        return lax.dot_general(kz_ref[h, b * BLOCK:(b + 3) * BLOCK, :], qst_ref[h, b],
                               (((1,), (1,)), ((), ())),
                               preferred_element_type=F32)

    def finish(b, h, s_t):
        base = i * tq + b * BLOCK
        mask_lo = (c_io >= r_io) & (c_io + (base - BLOCK) >= 0)
        mask_hi = (c_io <= r_io) & (c_io + (base + BLOCK) < seq_len)
        probs, sink_terms = [], []
        for j in range(grp):
            sink = sink_ref[grp * h + j] * LOG2E
            cols = slice(j * BLOCK, (j + 1) * BLOCK)
            lo = jnp.where(mask_lo, s_t[0:BLOCK, cols], -jnp.inf)
            mid = s_t[BLOCK:2 * BLOCK, cols]
            hi = jnp.where(mask_hi, s_t[2 * BLOCK:3 * BLOCK, cols], -jnp.inf)
            m = jnp.maximum(jnp.maximum(jnp.max(lo, axis=0, keepdims=True),
                                        jnp.max(hi, axis=0, keepdims=True)),
                            jnp.maximum(jnp.max(mid, axis=0, keepdims=True), sink))
            probs.append(jnp.concatenate(
                [jnp.exp2(lo - m), jnp.exp2(mid - m), jnp.exp2(hi - m)], axis=0).astype(BF16))
            sink_terms.append(jnp.exp2(sink - m))
        p_t = jnp.concatenate(probs, axis=1)
        o_t = jnp.dot(vt_ref[h, :, b * BLOCK:(b + 3) * BLOCK], p_t,
                      preferred_element_type=F32)
        outs = []
        for j in range(grp):
            cols = slice(j * BLOCK, (j + 1) * BLOCK)
            denom = o_t[HEAD_DIM:HEAD_DIM + 1, cols] + sink_terms[j]
            outs.append(o_t[:HEAD_DIM, cols] * (1.0 / denom))
        for pr in range(grp // 2):
            pair_t = jnp.concatenate([outs[2 * pr], outs[2 * pr + 1]], axis=0)
            g = (grp // 2) * h + pr
            att_s_ref[slot, b * BLOCK:(b + 1) * BLOCK, g * LANES:(g + 1) * LANES] = (
                pair_t.T.astype(BF16))

    units = [(b, h) for b in range(nb) for h in range(N_KV_HEADS)]
    n_units, n_pieces = len(units), len(merge_pieces)
    s_next = scores(*units[0])
    for u, (b, h) in enumerate(units):
        s_t = s_next
        if u + 1 < n_units:
            s_next = scores(*units[u + 1])
        for piece in merge_pieces[u * n_pieces // n_units:(u + 1) * n_pieces // n_units]:
            piece()
        finish(b, h, s_t)

    zero8 = jnp.zeros((F32_ROWS, LANES), F32)
    for g in range(D_CONV // LANES):
        cols = slice(g * LANES, (g + 1) * LANES)
        zc = slice(OFF_C + g * LANES, OFF_C + (g + 1) * LANES)
        zv = slice(OFF_VA + g * LANES, OFF_VA + (g + 1) * LANES)
        zb = slice(OFF_B + g * LANES, OFF_B + (g + 1) * LANES)
        prev = (cp_ref[:, cols].astype(F32) * vp_ref[:, cols].astype(F32))[F32_ROWS:, :]
        nxt = (cn_ref[:, cols].astype(F32) * vn_ref[:, cols].astype(F32))[:F32_ROWS, :]
        cvs_ref[0:F32_ROWS, cols] = jnp.where(i > 0, prev, zero8)
        for r0 in range(0, tq, ROW_CHUNK):
            rows = slice(r0, r0 + ROW_CHUNK)
            cvs_ref[F32_ROWS + r0:F32_ROWS + r0 + ROW_CHUNK, cols] = (
                z_ref[rows, zc].astype(F32) * z_ref[rows, zv].astype(F32))
        cvs_ref[F32_ROWS + tq:2 * F32_ROWS + tq, cols] = jnp.where(i < nt - 1, nxt, zero8)
        for r0 in range(0, tq, ROW_CHUNK):
            lo = F32_ROWS + r0
            conv = (cvs_ref[lo - 1:lo - 1 + ROW_CHUNK, cols] * cw_ref[0:1, cols]
                    + cvs_ref[lo:lo + ROW_CHUNK, cols] * cw_ref[1:2, cols]
                    + cvs_ref[lo + 1:lo + 1 + ROW_CHUNK, cols] * cw_ref[2:3, cols])
            ya_s_ref[slot, r0:r0 + ROW_CHUNK, cols] = (
                z_ref[r0:r0 + ROW_CHUNK, zb].astype(F32) * conv).astype(BF16)


def _mixer(x, z, cos_t, sin_t, sink, conv_w, b_gate, w_out_a, w_o, w_mix, cast_weights):
    s, d = x.shape
    tq = MIX_TQ
    nt = s // tq
    kvb = tq // BLOCK
    cb = tq // BF16_ROWS
    qbcv = 4 * D_CONV
    kv_w = 2 * D_KV
    n_col = d // COL_CHUNK
    assert OFF_B == 0 and OFF_Q + D_ATTN == qbcv and OFF_K % kv_w == 0 and OFF_V == OFF_K + D_KV
    assert OFF_GA % COL_CHUNK == 0 and OFF_GB % COL_CHUNK == 0

    def br(step):
        return jnp.minimum(step, nt - 1)

    def mg(step):
        return jnp.maximum(step - 1, 0)

    def prev_blk(step, per):
        return jnp.maximum(br(step) * per - 1, 0)

    def next_blk(step, per, total):
        return jnp.minimum((br(step) + 1) * per, total - 1)

    def gate_spec(off, c):
        return pl.BlockSpec((tq, COL_CHUNK), lambda i: (mg(i), off // COL_CHUNK + c))

    const = lambda i: (0, 0)
    single = dict(pipeline_mode=pl.Buffered(1))
    in_specs = [
        pl.BlockSpec(memory_space=pltpu.SMEM),
        pl.BlockSpec((tq, qbcv), lambda i: (br(i), 0)),
        pl.BlockSpec((tq, kv_w), lambda i: (br(i), OFF_K // kv_w)),
        pl.BlockSpec((BLOCK, kv_w), lambda i: (prev_blk(i, kvb), OFF_K // kv_w)),
        pl.BlockSpec((BLOCK, kv_w), lambda i: (next_blk(i, kvb, s // BLOCK), OFF_K // kv_w)),
        pl.BlockSpec((BF16_ROWS, D_CONV), lambda i: (prev_blk(i, cb), OFF_C // D_CONV)),
        pl.BlockSpec((BF16_ROWS, D_CONV), lambda i: (prev_blk(i, cb), OFF_VA // D_CONV)),
        pl.BlockSpec((BF16_ROWS, D_CONV), lambda i: (next_blk(i, cb, s // BF16_ROWS), OFF_C // D_CONV)),
        pl.BlockSpec((BF16_ROWS, D_CONV), lambda i: (next_blk(i, cb, s // BF16_ROWS), OFF_VA // D_CONV)),
        pl.BlockSpec((tq, LANES), lambda i: (br(i), 0)),
        pl.BlockSpec((tq, LANES), lambda i: (br(i), 0)),
        pl.BlockSpec((BLOCK, LANES), lambda i: (prev_blk(i, kvb), 0)),
        pl.BlockSpec((BLOCK, LANES), lambda i: (prev_blk(i, kvb), 0)),
        pl.BlockSpec((BLOCK, LANES), lambda i: (next_blk(i, kvb, s // BLOCK), 0)),
        pl.BlockSpec((BLOCK, LANES), lambda i: (next_blk(i, kvb, s // BLOCK), 0)),
        pl.BlockSpec((3, D_CONV), const),
        pl.BlockSpec((tq, d), lambda i: (mg(i), 0)),
    ] + [gate_spec(OFF_GA, c) for c in range(n_col)] + [
        gate_spec(OFF_GB, c) for c in range(n_col)] + [
        pl.BlockSpec((1, 2 * d), const),
        pl.BlockSpec((D_CONV, d), const, **single),
        pl.BlockSpec((D_ATTN, d), const, **single),
        pl.BlockSpec((d, d), const, **single),
    ]
    r_ext = tq + 2 * BLOCK
    grp = N_HEADS // N_KV_HEADS
    slabs = [_slab_spec(cw.shape, nt, br) for cw in cast_weights]
    outs = pl.pallas_call(
        functools.partial(_mixer_kernel, seq_len=s, n_cast=len(cast_weights)),
        grid=(nt + 1,),
        in_specs=in_specs + slabs,
        out_specs=[pl.BlockSpec((tq, d), lambda i: (mg(i), 0))] + slabs,
        out_shape=[jax.ShapeDtypeStruct((s, d), F32)]
        + [jax.ShapeDtypeStruct(cw.shape, BF16) for cw in cast_weights],
        scratch_shapes=[
            pltpu.VMEM((2, tq, D_ATTN), BF16),
            pltpu.VMEM((2, tq, D_CONV), BF16),
            pltpu.VMEM((N_KV_HEADS, tq // BLOCK, grp * BLOCK, LANES), BF16),
            pltpu.VMEM((N_KV_HEADS, r_ext, LANES), BF16),
            pltpu.VMEM((N_KV_HEADS, HEAD_DIM + BF16_ROWS, r_ext), BF16),
            pltpu.VMEM((tq + 2 * F32_ROWS, D_CONV), F32),
            pltpu.VMEM((2, tq, COL_CHUNK), F32),
            pltpu.VMEM((2, tq, COL_CHUNK), F32),
            pltpu.VMEM((tq, d), BF16),
        ],
        compiler_params=pltpu.CompilerParams(
            dimension_semantics=("arbitrary",),
            vmem_limit_bytes=MIX_VMEM_LIMIT_BYTES),
        name="mixer",
    )(sink, z, z, z, z, z, z, z, z, cos_t, sin_t, cos_t, sin_t, cos_t, sin_t, conv_w, x,
      *([z] * (2 * n_col)), b_gate, w_out_a, w_o, w_mix, *cast_weights)
    return outs[0], outs[1:]


def _ffn_kernel(h_ref, hp_ref, hn_ref, g2_ref, wa_ref, wg_ref, cwa_ref, cwg_ref,
                ba_ref, bgt_ref, wd_ref, gf_ref,
                o_ref,
                u_ref, *sub_refs, final_norm):
    i = pl.program_id(0)
    j = pl.program_id(1)
    nt = pl.num_programs(0)
    nf = pl.num_programs(1)
    tm = h_ref.shape[0]
    halo = BF16_ROWS
    n_sub = len(sub_refs) // 3
    sa_refs, sg_refs, act_refs = sub_refs[0::3], sub_refs[1::3], sub_refs[2::3]

    @pl.when(j == 0)
    def _():
        g2 = g2_ref[...]
        zero = jnp.zeros((halo, h_ref.shape[1]), F32)
        u_ref[0:halo, :] = jnp.where(i > 0, _rmsnorm(hp_ref[...], g2), zero).astype(BF16)
        u_ref[halo + tm:2 * halo + tm, :] = jnp.where(
            i < nt - 1, _rmsnorm(hn_ref[...], g2), zero).astype(BF16)

        def body(r, carry):
            r0 = pl.multiple_of(r * ROW_CHUNK, ROW_CHUNK)
            u_ref[pl.ds(halo + r0, ROW_CHUNK), :] = _rmsnorm(
                h_ref[pl.ds(r0, ROW_CHUNK), :], g2).astype(BF16)
            o_ref[pl.ds(r0, ROW_CHUNK), :] = jnp.zeros((ROW_CHUNK, o_ref.shape[1]), F32)
            return carry
        lax.fori_loop(0, tm // ROW_CHUNK, body, 0)

    def conv(s_ref, cw_ref, b_ref, r0, cs):
        return (s_ref[r0 + halo - 1:r0 + halo - 1 + FFN_ACT_ROWS, :] * cw_ref[0:1, cs]
                + s_ref[r0 + halo:r0 + halo + FFN_ACT_ROWS, :] * cw_ref[1:2, cs]
                + s_ref[r0 + halo + 1:r0 + halo + 1 + FFN_ACT_ROWS, :] * cw_ref[2:3, cs]) + b_ref[:, cs]

    def sub_cols(k):
        return slice(k * FFN_SUB, (k + 1) * FFN_SUB)

    def up(k, w_ref, s_refs):
        rows_all = tm + 2 * halo
        for r0 in range(0, rows_all, rows_all // FFN_UP_SPLIT):
            rows = slice(r0, r0 + rows_all // FFN_UP_SPLIT)
            s_refs[k][rows, :] = jnp.dot(u_ref[rows, :], w_ref[:, sub_cols(k)],
                                         preferred_element_type=F32)

    def activate(k):
        for r0 in range(0, tm, FFN_ACT_ROWS):
            a = conv(sa_refs[k], cwa_ref, ba_ref, r0, sub_cols(k))
            g = conv(sg_refs[k], cwg_ref, bgt_ref, r0, sub_cols(k))
            act_refs[k][r0:r0 + FFN_ACT_ROWS, :] = (a * jax.nn.sigmoid(a) * g).astype(BF16)

    def down(k):
        for c in range(o_ref.shape[1] // COL_CHUNK):
            cols = slice(c * COL_CHUNK, (c + 1) * COL_CHUNK)
            o_ref[:, cols] += jnp.dot(act_refs[k][...], wd_ref[sub_cols(k), cols],
                                      preferred_element_type=F32)

    for k in range(n_sub):
        up(k, wa_ref, sa_refs)
        up(k, wg_ref, sg_refs)
    for k in range(n_sub):
        activate(k)
        down(k)

    @pl.when(j == nf - 1)
    def _():
        def body(r, carry):
            rows = pl.ds(pl.multiple_of(r * ROW_CHUNK, ROW_CHUNK), ROW_CHUNK)
            h = h_ref[rows, :] + o_ref[rows, :]
            if final_norm:
                h = _rmsnorm(h, gf_ref[...])
            o_ref[rows, :] = h
            return carry
        lax.fori_loop(0, tm // ROW_CHUNK, body, 0)


def _ffn(h, g2, w_up, conv_w, conv_b, w_down, gf, final_norm):
    s, d = h.shape
    tm, tf = FFN_TM, FFN_TF
    nt, nf = s // tm, D_FF // tf
    hb = tm // BF16_ROWS
    n_sub = tf // FFN_SUB

    in_specs = [
        pl.BlockSpec((tm, d), lambda i, j: (i, 0)),
        pl.BlockSpec((BF16_ROWS, d), lambda i, j: (jnp.maximum(i * hb - 1, 0), 0)),
        pl.BlockSpec((BF16_ROWS, d), lambda i, j: (jnp.minimum((i + 1) * hb, s // BF16_ROWS - 1), 0)),
        pl.BlockSpec((1, d), lambda i, j: (0, 0)),
        pl.BlockSpec((d, tf), lambda i, j: (0, j)),
        pl.BlockSpec((d, tf), lambda i, j: (0, nf + j)),
        pl.BlockSpec((3, tf), lambda i, j: (0, j)),
        pl.BlockSpec((3, tf), lambda i, j: (0, nf + j)),
        pl.BlockSpec((1, tf), lambda i, j: (0, j)),
        pl.BlockSpec((1, tf), lambda i, j: (0, nf + j)),
        pl.BlockSpec((tf, d), lambda i, j: (j, 0)),
        pl.BlockSpec((1, d), lambda i, j: (0, 0)),
    ]
    return pl.pallas_call(
        functools.partial(_ffn_kernel, final_norm=final_norm),
        grid=(nt, nf),
        in_specs=in_specs,
        out_specs=pl.BlockSpec((tm, d), lambda i, j: (i, 0)),
        out_shape=jax.ShapeDtypeStruct((s, d), F32),
        scratch_shapes=[
            pltpu.VMEM((tm + 2 * BF16_ROWS, d), BF16),
        ] + [
            pltpu.VMEM((tm + 2 * BF16_ROWS, FFN_SUB), F32),
            pltpu.VMEM((tm + 2 * BF16_ROWS, FFN_SUB), F32),
            pltpu.VMEM((tm, FFN_SUB), BF16),
        ] * n_sub,
        compiler_params=pltpu.CompilerParams(
            dimension_semantics=("arbitrary", "arbitrary"),
            vmem_limit_bytes=VMEM_LIMIT_BYTES),
        name="ffn",
    )(h, h, h, g2, w_up, w_up, conv_w, conv_w, conv_b, conv_b, w_down, gf)


def _rope_tables(seq_len):
    half = HEAD_DIM // 2
    reps = LANES // half
    inv_freq = jnp.tile(ROPE_THETA ** (-jnp.arange(0, half, dtype=F32) / half), reps)[None, :]
    sign = jnp.tile(jnp.concatenate([-jnp.ones((half,), F32), jnp.ones((half,), F32)]),
                    reps // 2)
    base = (ROPE_SPAN * jnp.arange(seq_len // ROPE_SPAN, dtype=F32))[:, None] * inv_freq
    offs = jnp.arange(ROPE_SPAN, dtype=F32)[:, None] * inv_freq
    cb, sb = jnp.cos(base)[:, None, :], jnp.sin(base)[:, None, :]
    co, so = jnp.cos(offs)[None, :, :], jnp.sin(offs)[None, :, :]
    cos_t = (cb * co - sb * so).reshape(seq_len, LANES)
    sin_t = ((sb * co + cb * so) * sign).reshape(seq_len, LANES)
    return cos_t, sin_t


def kernel(x, norm_mix_g, w_in, b_gate, conv_a_w, w_out_a, sink_logits, w_o_attn, w_mix_out,
           norm_ffn_g, ffn_w_up, ffn_conv_w, ffn_conv_b, ffn_w_down, norm_final_g):
    b, s, d = x.shape
    depth = w_in.shape[0]
    cos_t, sin_t = _rope_tables(s)
    outs = []
    for bi in range(b):
        h = x[bi]
        for l in range(depth):
            z, (w_oa, w_o, w_mix) = _in_proj(
                h, norm_mix_g[l][None, :], w_in[l].astype(BF16),
                (w_out_a[l], w_o_attn[l], w_mix_out[l]))
            h, (w_up, w_down) = _mixer(
                h, z, cos_t, sin_t, sink_logits[l], conv_a_w[l], b_gate[l][None, :],
                w_oa, w_o, w_mix, (ffn_w_up[l], ffn_w_down[l]))
            h = _ffn(h, norm_ffn_g[l][None, :], w_up, ffn_conv_w[l], ffn_conv_b[l][None, :],
                     w_down, norm_final_g[None, :], final_norm=(l == depth - 1))
        outs.append(h[None])
    return outs[0] if b == 1 else jnp.concatenate(outs, axis=0)
```

```python
import functools

import jax
import jax.numpy as jnp
from jax import lax
from jax.experimental import pallas as pl
from jax.experimental.pallas import tpu as pltpu

D_MODEL = 2048
D_CONV = D_MODEL // 2
N_HEADS = 16
N_KV_HEADS = 4
HEAD_DIM = 64
D_ATTN = N_HEADS * HEAD_DIM
D_KV = N_KV_HEADS * HEAD_DIM
WINDOW = 128
BLOCK = 128
ROPE_THETA = 10000.0
D_FF = 5632
EPS = 1e-6

OFF_B = 0
OFF_C = OFF_B + D_CONV
OFF_VA = OFF_C + D_CONV
OFF_Q = OFF_VA + D_CONV
OFF_K = OFF_Q + D_ATTN
OFF_V = OFF_K + D_KV
OFF_GA = OFF_V + D_KV
OFF_GB = OFF_GA + D_MODEL
D_IN_PROJ = OFF_GB + D_MODEL
LOG2E = 1.4426950408889634
ROPE_SPAN = 256

LANES = 128
BF16_ROWS = 16
F32_ROWS = 8
VMEM_LIMIT_BYTES = 60000 * 1024
IN_VMEM_LIMIT_BYTES = 57 * 1024 * 1024
MIX_VMEM_LIMIT_BYTES = 52 * 1024 * 1024

IN_TM = 512
IN_TN = 4352
MIX_TQ = 256
MIX_OUT_COLS = 256
FFN_TM = 1024
FFN_TF = 512
FFN_SUB = 256
FFN_ACT_ROWS = 128
FFN_UP_SPLIT = 2
ROW_CHUNK = 128
COL_CHUNK = 512

BF16 = jnp.bfloat16
F32 = jnp.float32


def _rmsnorm(x, g):
    ms = jnp.mean(x * x, axis=-1, keepdims=True)
    return x * lax.rsqrt(ms + EPS) * g


def _in_proj_kernel(x_ref, g_ref, w_ref, *rest, n_cast):
    src_refs = rest[:n_cast]
    z_ref = rest[n_cast]
    dst_refs = rest[n_cast + 1:2 * n_cast + 1]
    u_ref = rest[2 * n_cast + 1]

    @pl.when(pl.program_id(1) == 0)
    def _():
        def body(r, carry):
            rows = pl.ds(pl.multiple_of(r * ROW_CHUNK, ROW_CHUNK), ROW_CHUNK)
            u_ref[rows, :] = _rmsnorm(x_ref[rows, :], g_ref[...]).astype(BF16)
            return carry
        lax.fori_loop(0, x_ref.shape[0] // ROW_CHUNK, body, 0)

    for src, dst in zip(src_refs, dst_refs):
        dst[...] = src[...].astype(dst.dtype)

    z_ref[...] = jnp.dot(u_ref[...], w_ref[...],
                         preferred_element_type=F32).astype(z_ref.dtype)


def _slab_spec(shape, n_steps, step_fn):
    rows, cols = shape
    per = rows // n_steps
    hold = 1
    while per % BF16_ROWS:
        per, hold = per * 2, hold * 2
    assert per * (n_steps // hold) == rows
    return pl.BlockSpec((per, cols), lambda *idx: (step_fn(*idx) // hold, 0))


def _in_proj(x, g, w, cast_weights):
    s, d = x.shape
    n = w.shape[1]
    grid = (s // IN_TM, n // IN_TN)
    slabs = [_slab_spec(cw.shape, grid[0] * grid[1], lambda i, j: i * grid[1] + j)
             for cw in cast_weights]
    outs = pl.pallas_call(
        functools.partial(_in_proj_kernel, n_cast=len(cast_weights)),
        grid=grid,
        in_specs=[
            pl.BlockSpec((IN_TM, d), lambda i, j: (i, 0)),
            pl.BlockSpec((1, d), lambda i, j: (0, 0)),
            pl.BlockSpec((d, IN_TN), lambda i, j: (0, j)),
        ] + slabs,
        out_specs=[pl.BlockSpec((IN_TM, IN_TN), lambda i, j: (i, j))] + slabs,
        out_shape=[jax.ShapeDtypeStruct((s, n), BF16)]
        + [jax.ShapeDtypeStruct(cw.shape, BF16) for cw in cast_weights],
        scratch_shapes=[pltpu.VMEM((IN_TM, d), BF16)],
        compiler_params=pltpu.CompilerParams(
            dimension_semantics=("arbitrary", "arbitrary"),
            vmem_limit_bytes=IN_VMEM_LIMIT_BYTES),
        name="in_proj",
    )(x, g, w, *cast_weights)
    return outs[0], outs[1:]


def _rope(x, cos, sin_signed, first_half):
    partner = jnp.where(first_half, pltpu.roll(x, LANES - HEAD_DIM // 2, 1),
                        pltpu.roll(x, HEAD_DIM // 2, 1))
    return x * cos + partner * sin_signed


def _mixer_kernel(sink_ref, z_ref, kv_ref, kvp_ref, kvn_ref, cp_ref, vp_ref, cn_ref, vn_ref,
                  cos_ref, sin_ref, cosp_ref, sinp_ref, cosn_ref, sinn_ref, cw_ref, x_ref,
                  *rest, seq_len, n_cast):
    n_col = x_ref.shape[1] // COL_CHUNK
    gla_refs, glb_refs = rest[:n_col], rest[n_col:2 * n_col]
    bg_ref, woa_ref, wo_ref, wmix_ref = rest[2 * n_col:2 * n_col + 4]
    rest = rest[2 * n_col + 4:]
    src_refs = rest[:n_cast]
    h1_ref = rest[n_cast]
    dst_refs = rest[n_cast + 1:2 * n_cast + 1]
    (att_s_ref, ya_s_ref, qst_ref, kz_ref, vt_ref, cvs_ref,
     ys_a_ref, ys_b_ref, mix_ref) = rest[2 * n_cast + 1:]
    for src, dst in zip(src_refs, dst_refs):
        dst[...] = src[...].astype(dst.dtype)

    step = pl.program_id(0)
    nt = pl.num_programs(0) - 1
    i = jnp.minimum(step, nt - 1)
    slot = step % 2
    pslot = 1 - slot
    tq = z_ref.shape[0]
    d = x_ref.shape[1]

    @pl.when(step == 0)
    def _():
        att_s_ref[1] = jnp.zeros(att_s_ref.shape[1:], BF16)
        ya_s_ref[1] = jnp.zeros(ya_s_ref.shape[1:], BF16)

    def merge_stage1(c):
        cols = slice(c * COL_CHUNK, (c + 1) * COL_CHUNK)
        gcols = slice(d + c * COL_CHUNK, d + (c + 1) * COL_CHUNK)
        ys_a_ref[c % 2] = jnp.dot(ya_s_ref[pslot], woa_ref[:, cols], preferred_element_type=F32)
        ys_b_ref[c % 2] = jnp.dot(att_s_ref[pslot], wo_ref[:, cols], preferred_element_type=F32)
        for r0 in range(0, tq, ROW_CHUNK):
            rows = slice(r0, r0 + ROW_CHUNK)
            g_a = jax.nn.sigmoid(gla_refs[c][rows, :].astype(F32) + bg_ref[:, cols])
            g_b = jax.nn.sigmoid(glb_refs[c][rows, :].astype(F32) + bg_ref[:, gcols])
            mix_ref[rows, cols] = (g_a * ys_a_ref[c % 2, rows, :]
                                   + g_b * ys_b_ref[c % 2, rows, :]).astype(BF16)

    def merge_stage2(c):
        cols = slice(c * MIX_OUT_COLS, (c + 1) * MIX_OUT_COLS)
        h1_ref[:, cols] = x_ref[:, cols] + jnp.dot(mix_ref[...], wmix_ref[:, cols],
                                                   preferred_element_type=F32)

    for c in range(n_col):
        merge_stage1(c)
    merge_pieces = [functools.partial(merge_stage2, c) for c in range(d // MIX_OUT_COLS)]

    nb = tq // BLOCK
    grp = N_HEADS // N_KV_HEADS
    lane = lax.broadcasted_iota(jnp.int32, (1, LANES), 1)
    first_half = (lane % HEAD_DIM) < (HEAD_DIM // 2)
    low_head = lane < HEAD_DIM
    scale = HEAD_DIM ** -0.5 * LOG2E

    for b in range(nb):
        rows = slice(b * BLOCK, (b + 1) * BLOCK)
        cos_b = cos_ref[rows, :]
        sin_b = sin_ref[rows, :]
        for g in range(D_ATTN // LANES):
            h = g // 2
            xq = z_ref[rows, OFF_Q + g * LANES:OFF_Q + (g + 1) * LANES].astype(F32)
            xr = _rope(xq, cos_b, sin_b, first_half) * scale
            rolled = pltpu.roll(xr, HEAD_DIM, 1)
            even, odd = (xr, rolled) if h % 2 == 0 else (rolled, xr)
            j0 = 2 * (g % 2)
            qst_ref[h, b, j0 * BLOCK:(j0 + 1) * BLOCK, :] = even.astype(BF16)
            qst_ref[h, b, (j0 + 1) * BLOCK:(j0 + 2) * BLOCK, :] = odd.astype(BF16)

    sub = lax.broadcasted_iota(jnp.int32, (BF16_ROWS, BLOCK), 0)
    ones_rows = jnp.where(sub == 0, 1.0, 0.0).astype(F32)

    def put_kv(kv_blk_ref, c_ref, s_ref, blk, row0):
        src = slice(blk * BLOCK, (blk + 1) * BLOCK)
        dst = slice(row0, row0 + BLOCK)
        c = c_ref[src, :]
        s = s_ref[src, :]
        for pr in range(N_KV_HEADS // 2):
            kf = _rope(kv_blk_ref[src, pr * LANES:(pr + 1) * LANES].astype(F32), c, s, first_half)
            zero = jnp.zeros_like(kf)
            kz_ref[2 * pr, dst, :] = jnp.where(low_head, kf, zero).astype(BF16)
            kz_ref[2 * pr + 1, dst, :] = jnp.where(low_head, zero, kf).astype(BF16)
            vb_t = kv_blk_ref[src, D_KV + pr * LANES:D_KV + (pr + 1) * LANES].astype(F32).T
            vt_ref[2 * pr, :, dst] = jnp.concatenate(
                [vb_t[:HEAD_DIM], ones_rows], axis=0).astype(BF16)
            vt_ref[2 * pr + 1, :, dst] = jnp.concatenate(
                [vb_t[HEAD_DIM:], ones_rows], axis=0).astype(BF16)

    put_kv(kvp_ref, cosp_ref, sinp_ref, 0, 0)
    for blk in range(nb):
        put_kv(kv_ref, cos_ref, sin_ref, blk, (blk + 1) * BLOCK)
    put_kv(kvn_ref, cosn_ref, sinn_ref, 0, BLOCK + tq)

    c_io = lax.broadcasted_iota(jnp.int32, (BLOCK, BLOCK), 0)
    r_io = lax.broadcasted_iota(jnp.int32, (BLOCK, BLOCK), 1)

    def scores(b, h):
        return lax.dot_general(kz_ref[h, b * BLOCK:(b + 3) * BLOCK, :], qst_ref[h, b],
                               (((1,), (1,)), ((), ())),
                               preferred_element_type=F32)

    def finish(b, h, s_t):
        base = i * tq + b * BLOCK
        mask_lo = (c_io >= r_io) & (c_io + (base - BLOCK) >= 0)
        mask_hi = (c_io <= r_io) & (c_io + (base + BLOCK) < seq_len)
        probs, sink_terms = [], []
        for j in range(grp):
            sink = sink_ref[grp * h + j] * LOG2E
            cols = slice(j * BLOCK, (j + 1) * BLOCK)
            lo = jnp.where(mask_lo, s_t[0:BLOCK, cols], -jnp.inf)
            mid = s_t[BLOCK:2 * BLOCK, cols]
            hi = jnp.where(mask_hi, s_t[2 * BLOCK:3 * BLOCK, cols], -jnp.inf)
            m = jnp.maximum(jnp.maximum(jnp.max(lo, axis=0, keepdims=True),
                                        jnp.max(hi, axis=0, keepdims=True)),
                            jnp.maximum(jnp.max(mid, axis=0, keepdims=True), sink))
            probs.append(jnp.concatenate(
                [jnp.exp2(lo - m), jnp.exp2(mid - m), jnp.exp2(hi - m)], axis=0).astype(BF16))
            sink_terms.append(jnp.exp2(sink - m))
        p_t = jnp.concatenate(probs, axis=1)
        o_t = jnp.dot(vt_ref[h, :, b * BLOCK:(b + 3) * BLOCK], p_t,
                      preferred_element_type=F32)
        outs = []
        for j in range(grp):
            cols = slice(j * BLOCK, (j + 1) * BLOCK)
            denom = o_t[HEAD_DIM:HEAD_DIM + 1, cols] + sink_terms[j]
            outs.append(o_t[:HEAD_DIM, cols] * (1.0 / denom))
        for pr in range(grp // 2):
            pair_t = jnp.concatenate([outs[2 * pr], outs[2 * pr + 1]], axis=0)
            g = (grp // 2) * h + pr
            att_s_ref[slot, b * BLOCK:(b + 1) * BLOCK, g * LANES:(g + 1) * LANES] = (
                pair_t.T.astype(BF16))

    units = [(b, h) for b in range(nb) for h in range(N_KV_HEADS)]
    n_units, n_pieces = len(units), len(merge_pieces)
    s_next = scores(*units[0])
    for u, (b, h) in enumerate(units):
        s_t = s_next
        if u + 1 < n_units:
            s_next = scores(*units[u + 1])
        for piece in merge_pieces[u * n_pieces // n_units:(u + 1) * n_pieces // n_units]:
            piece()
        finish(b, h, s_t)

    zero8 = jnp.zeros((F32_ROWS, LANES), F32)
    for g in range(D_CONV // LANES):
        cols = slice(g * LANES, (g + 1) * LANES)
        zc = slice(OFF_C + g * LANES, OFF_C + (g + 1) * LANES)
        zv = slice(OFF_VA + g * LANES, OFF_VA + (g + 1) * LANES)
        zb = slice(OFF_B + g * LANES, OFF_B + (g + 1) * LANES)
        prev = (cp_ref[:, cols].astype(F32) * vp_ref[:, cols].astype(F32))[F32_ROWS:, :]
        nxt = (cn_ref[:, cols].astype(F32) * vn_ref[:, cols].astype(F32))[:F32_ROWS, :]
        cvs_ref[0:F32_ROWS, cols] = jnp.where(i > 0, prev, zero8)
        for r0 in range(0, tq, ROW_CHUNK):
            rows = slice(r0, r0 + ROW_CHUNK)
            cvs_ref[F32_ROWS + r0:F32_ROWS + r0 + ROW_CHUNK, cols] = (
                z_ref[rows, zc].astype(F32) * z_ref[rows, zv].astype(F32))
        cvs_ref[F32_ROWS + tq:2 * F32_ROWS + tq, cols] = jnp.where(i < nt - 1, nxt, zero8)
        for r0 in range(0, tq, ROW_CHUNK):
            lo = F32_ROWS + r0
            conv = (cvs_ref[lo - 1:lo - 1 + ROW_CHUNK, cols] * cw_ref[0:1, cols]
                    + cvs_ref[lo:lo + ROW_CHUNK, cols] * cw_ref[1:2, cols]
                    + cvs_ref[lo + 1:lo + 1 + ROW_CHUNK, cols] * cw_ref[2:3, cols])
            ya_s_ref[slot, r0:r0 + ROW_CHUNK, cols] = (
                z_ref[r0:r0 + ROW_CHUNK, zb].astype(F32) * conv).astype(BF16)


def _mixer(x, z, cos_t, sin_t, sink, conv_w, b_gate, w_out_a, w_o, w_mix, cast_weights):
    s, d = x.shape
    tq = MIX_TQ
    nt = s // tq
    kvb = tq // BLOCK
    cb = tq // BF16_ROWS
    qbcv = 4 * D_CONV
    kv_w = 2 * D_KV
    n_col = d // COL_CHUNK
    assert OFF_B == 0 and OFF_Q + D_ATTN == qbcv and OFF_K % kv_w == 0 and OFF_V == OFF_K + D_KV
    assert OFF_GA % COL_CHUNK == 0 and OFF_GB % COL_CHUNK == 0

    def br(step):
        return jnp.minimum(step, nt - 1)

    def mg(step):
        return jnp.maximum(step - 1, 0)

    def prev_blk(step, per):
        return jnp.maximum(br(step) * per - 1, 0)

    def next_blk(step, per, total):
        return jnp.minimum((br(step) + 1) * per, total - 1)

    def gate_spec(off, c):
        return pl.BlockSpec((tq, COL_CHUNK), lambda i: (mg(i), off // COL_CHUNK + c))

    const = lambda i: (0, 0)
    single = dict(pipeline_mode=pl.Buffered(1))
    in_specs = [
        pl.BlockSpec(memory_space=pltpu.SMEM),
        pl.BlockSpec((tq, qbcv), lambda i: (br(i), 0)),
        pl.BlockSpec((tq, kv_w), lambda i: (br(i), OFF_K // kv_w)),
        pl.BlockSpec((BLOCK, kv_w), lambda i: (prev_blk(i, kvb), OFF_K // kv_w)),
        pl.BlockSpec((BLOCK, kv_w), lambda i: (next_blk(i, kvb, s // BLOCK), OFF_K // kv_w)),
        pl.BlockSpec((BF16_ROWS, D_CONV), lambda i: (prev_blk(i, cb), OFF_C // D_CONV)),
        pl.BlockSpec((BF16_ROWS, D_CONV), lambda i: (prev_blk(i, cb), OFF_VA // D_CONV)),
        pl.BlockSpec((BF16_ROWS, D_CONV), lambda i: (next_blk(i, cb, s // BF16_ROWS), OFF_C // D_CONV)),
        pl.BlockSpec((BF16_ROWS, D_CONV), lambda i: (next_blk(i, cb, s // BF16_ROWS), OFF_VA // D_CONV)),
        pl.BlockSpec((tq, LANES), lambda i: (br(i), 0)),
        pl.BlockSpec((tq, LANES), lambda i: (br(i), 0)),
        pl.BlockSpec((BLOCK, LANES), lambda i: (prev_blk(i, kvb), 0)),
        pl.BlockSpec((BLOCK, LANES), lambda i: (prev_blk(i, kvb), 0)),
        pl.BlockSpec((BLOCK, LANES), lambda i: (next_blk(i, kvb, s // BLOCK), 0)),
        pl.BlockSpec((BLOCK, LANES), lambda i: (next_blk(i, kvb, s // BLOCK), 0)),
        pl.BlockSpec((3, D_CONV), const),
        pl.BlockSpec((tq, d), lambda i: (mg(i), 0)),
    ] + [gate_spec(OFF_GA, c) for c in range(n_col)] + [
        gate_spec(OFF_GB, c) for c in range(n_col)] + [
        pl.BlockSpec((1, 2 * d), const),
        pl.BlockSpec((D_CONV, d), const, **single),
        pl.BlockSpec((D_ATTN, d), const, **single),
        pl.BlockSpec((d, d), const, **single),
    ]
    r_ext = tq + 2 * BLOCK
    grp = N_HEADS // N_KV_HEADS
    slabs = [_slab_spec(cw.shape, nt, br) for cw in cast_weights]
    outs = pl.pallas_call(
        functools.partial(_mixer_kernel, seq_len=s, n_cast=len(cast_weights)),
        grid=(nt + 1,),
        in_specs=in_specs + slabs,
        out_specs=[pl.BlockSpec((tq, d), lambda i: (mg(i), 0))] + slabs,
        out_shape=[jax.ShapeDtypeStruct((s, d), F32)]
        + [jax.ShapeDtypeStruct(cw.shape, BF16) for cw in cast_weights],
        scratch_shapes=[
            pltpu.VMEM((2, tq, D_ATTN), BF16),
            pltpu.VMEM((2, tq, D_CONV), BF16),
            pltpu.VMEM((N_KV_HEADS, tq // BLOCK, grp * BLOCK, LANES), BF16),
            pltpu.VMEM((N_KV_HEADS, r_ext, LANES), BF16),
            pltpu.VMEM((N_KV_HEADS, HEAD_DIM + BF16_ROWS, r_ext), BF16),
            pltpu.VMEM((tq + 2 * F32_ROWS, D_CONV), F32),
            pltpu.VMEM((2, tq, COL_CHUNK), F32),
            pltpu.VMEM((2, tq, COL_CHUNK), F32),
            pltpu.VMEM((tq, d), BF16),
        ],
        compiler_params=pltpu.CompilerParams(
            dimension_semantics=("arbitrary",),
            vmem_limit_bytes=MIX_VMEM_LIMIT_BYTES),
        name="mixer",
    )(sink, z, z, z, z, z, z, z, z, cos_t, sin_t, cos_t, sin_t, cos_t, sin_t, conv_w, x,
      *([z] * (2 * n_col)), b_gate, w_out_a, w_o, w_mix, *cast_weights)
    return outs[0], outs[1:]


def _ffn_kernel(h_ref, hp_ref, hn_ref, g2_ref, wa_ref, wg_ref, cwa_ref, cwg_ref,
                ba_ref, bgt_ref, wd_ref, gf_ref,
                o_ref,
                u_ref, *sub_refs, final_norm):
    i = pl.program_id(0)
    j = pl.program_id(1)
    nt = pl.num_programs(0)
    nf = pl.num_programs(1)
    tm = h_ref.shape[0]
    halo = BF16_ROWS
    n_sub = len(sub_refs) // 3
    sa_refs, sg_refs, act_refs = sub_refs[0::3], sub_refs[1::3], sub_refs[2::3]

    @pl.when(j == 0)
    def _():
        g2 = g2_ref[...]
        zero = jnp.zeros((halo, h_ref.shape[1]), F32)
        u_ref[0:halo, :] = jnp.where(i > 0, _rmsnorm(hp_ref[...], g2), zero).astype(BF16)
        u_ref[halo + tm:2 * halo + tm, :] = jnp.where(
            i < nt - 1, _rmsnorm(hn_ref[...], g2), zero).astype(BF16)

        def body(r, carry):
            r0 = pl.multiple_of(r * ROW_CHUNK, ROW_CHUNK)
            u_ref[pl.ds(halo + r0, ROW_CHUNK), :] = _rmsnorm(
                h_ref[pl.ds(r0, ROW_CHUNK), :], g2).astype(BF16)
            o_ref[pl.ds(r0, ROW_CHUNK), :] = jnp.zeros((ROW_CHUNK, o_ref.shape[1]), F32)
            return carry
        lax.fori_loop(0, tm // ROW_CHUNK, body, 0)

    def conv(s_ref, cw_ref, b_ref, r0, cs):
        return (s_ref[r0 + halo - 1:r0 + halo - 1 + FFN_ACT_ROWS, :] * cw_ref[0:1, cs]
                + s_ref[r0 + halo:r0 + halo + FFN_ACT_ROWS, :] * cw_ref[1:2, cs]
                + s_ref[r0 + halo + 1:r0 + halo + 1 + FFN_ACT_ROWS, :] * cw_ref[2:3, cs]) + b_ref[:, cs]

    def sub_cols(k):
        return slice(k * FFN_SUB, (k + 1) * FFN_SUB)

    def up(k, w_ref, s_refs):
        rows_all = tm + 2 * halo
        for r0 in range(0, rows_all, rows_all // FFN_UP_SPLIT):
            rows = slice(r0, r0 + rows_all // FFN_UP_SPLIT)
            s_refs[k][rows, :] = jnp.dot(u_ref[rows, :], w_ref[:, sub_cols(k)],
                                         preferred_element_type=F32)

    def activate(k):
        for r0 in range(0, tm, FFN_ACT_ROWS):
            a = conv(sa_refs[k], cwa_ref, ba_ref, r0, sub_cols(k))
            g = conv(sg_refs[k], cwg_ref, bgt_ref, r0, sub_cols(k))
            act_refs[k][r0:r0 + FFN_ACT_ROWS, :] = (a * jax.nn.sigmoid(a) * g).astype(BF16)

    def down(k):
        for c in range(o_ref.shape[1] // COL_CHUNK):
            cols = slice(c * COL_CHUNK, (c + 1) * COL_CHUNK)
            o_ref[:, cols] += jnp.dot(act_refs[k][...], wd_ref[sub_cols(k), cols],
                                      preferred_element_type=F32)

    for k in range(n_sub):
        up(k, wa_ref, sa_refs)
        up(k, wg_ref, sg_refs)
    for k in range(n_sub):
        activate(k)
        down(k)

    @pl.when(j == nf - 1)
    def _():
        def body(r, carry):
            rows = pl.ds(pl.multiple_of(r * ROW_CHUNK, ROW_CHUNK), ROW_CHUNK)
            h = h_ref[rows, :] + o_ref[rows, :]
            if final_norm:
                h = _rmsnorm(h, gf_ref[...])
            o_ref[rows, :] = h
            return carry
        lax.fori_loop(0, tm // ROW_CHUNK, body, 0)


def _ffn(h, g2, w_up, conv_w, conv_b, w_down, gf, final_norm):
    s, d = h.shape
    tm, tf = FFN_TM, FFN_TF
    nt, nf = s // tm, D_FF // tf
    hb = tm // BF16_ROWS
    n_sub = tf // FFN_SUB

    in_specs = [
        pl.BlockSpec((tm, d), lambda i, j: (i, 0)),
        pl.BlockSpec((BF16_ROWS, d), lambda i, j: (jnp.maximum(i * hb - 1, 0), 0)),
        pl.BlockSpec((BF16_ROWS, d), lambda i, j: (jnp.minimum((i + 1) * hb, s // BF16_ROWS - 1), 0)),
        pl.BlockSpec((1, d), lambda i, j: (0, 0)),
        pl.BlockSpec((d, tf), lambda i, j: (0, j)),
        pl.BlockSpec((d, tf), lambda i, j: (0, nf + j)),
        pl.BlockSpec((3, tf), lambda i, j: (0, j)),
        pl.BlockSpec((3, tf), lambda i, j: (0, nf + j)),
        pl.BlockSpec((1, tf), lambda i, j: (0, j)),
        pl.BlockSpec((1, tf), lambda i, j: (0, nf + j)),
        pl.BlockSpec((tf, d), lambda i, j: (j, 0)),
        pl.BlockSpec((1, d), lambda i, j: (0, 0)),
    ]
    return pl.pallas_call(
        functools.partial(_ffn_kernel, final_norm=final_norm),
        grid=(nt, nf),
        in_specs=in_specs,
        out_specs=pl.BlockSpec((tm, d), lambda i, j: (i, 0)),
        out_shape=jax.ShapeDtypeStruct((s, d), F32),
        scratch_shapes=[
            pltpu.VMEM((tm + 2 * BF16_ROWS, d), BF16),
        ] + [
            pltpu.VMEM((tm + 2 * BF16_ROWS, FFN_SUB), F32),
            pltpu.VMEM((tm + 2 * BF16_ROWS, FFN_SUB), F32),
            pltpu.VMEM((tm, FFN_SUB), BF16),
        ] * n_sub,
        compiler_params=pltpu.CompilerParams(
            dimension_semantics=("arbitrary", "arbitrary"),
            vmem_limit_bytes=VMEM_LIMIT_BYTES),
        name="ffn",
    )(h, h, h, g2, w_up, w_up, conv_w, conv_w, conv_b, conv_b, w_down, gf)


def _rope_tables(seq_len):
    half = HEAD_DIM // 2
    reps = LANES // half
    inv_freq = jnp.tile(ROPE_THETA ** (-jnp.arange(0, half, dtype=F32) / half), reps)[None, :]
    sign = jnp.tile(jnp.concatenate([-jnp.ones((half,), F32), jnp.ones((half,), F32)]),
                    reps // 2)
    base = (ROPE_SPAN * jnp.arange(seq_len // ROPE_SPAN, dtype=F32))[:, None] * inv_freq
    offs = jnp.arange(ROPE_SPAN, dtype=F32)[:, None] * inv_freq
    cb, sb = jnp.cos(base)[:, None, :], jnp.sin(base)[:, None, :]
    co, so = jnp.cos(offs)[None, :, :], jnp.sin(offs)[None, :, :]
    cos_t = (cb * co - sb * so).reshape(seq_len, LANES)
    sin_t = ((sb * co + cb * so) * sign).reshape(seq_len, LANES)
    return cos_t, sin_t


def kernel(x, norm_mix_g, w_in, b_gate, conv_a_w, w_out_a, sink_logits, w_o_attn, w_mix_out,
           norm_ffn_g, ffn_w_up, ffn_conv_w, ffn_conv_b, ffn_w_down, norm_final_g):
    b, s, d = x.shape
    depth = w_in.shape[0]
    cos_t, sin_t = _rope_tables(s)
    outs = []
    for bi in range(b):
        h = x[bi]
        for l in range(depth):
            z, (w_oa, w_o, w_mix) = _in_proj(
                h, norm_mix_g[l][None, :], w_in[l].astype(BF16),
                (w_out_a[l], w_o_attn[l], w_mix_out[l]))
            h, (w_up, w_down) = _mixer(
                h, z, cos_t, sin_t, sink_logits[l], conv_a_w[l], b_gate[l][None, :],
                w_oa, w_o, w_mix, (ffn_w_up[l], ffn_w_down[l]))
            h = _ffn(h, norm_ffn_g[l][None, :], w_up, ffn_conv_w[l], ffn_conv_b[l][None, :],
                     w_down, norm_final_g[None, :], final_norm=(l == depth - 1))
        outs.append(h[None])
    return outs[0] if b == 1 else jnp.concatenate(outs, axis=0)
```

```python
import functools

import jax
import jax.numpy as jnp
from jax import lax
from jax.experimental import pallas as pl
from jax.experimental.pallas import tpu as pltpu

D_MODEL = 2048
D_CONV = D_MODEL // 2
N_HEADS = 16
N_KV_HEADS = 4
HEAD_DIM = 64
D_ATTN = N_HEADS * HEAD_DIM
D_KV = N_KV_HEADS * HEAD_DIM
WINDOW = 128
BLOCK = 128
ROPE_THETA = 10000.0
D_FF = 5632
EPS = 1e-6

OFF_B = 0
OFF_C = OFF_B + D_CONV
OFF_VA = OFF_C + D_CONV
OFF_Q = OFF_VA + D_CONV
OFF_K = OFF_Q + D_ATTN
OFF_V = OFF_K + D_KV
OFF_GA = OFF_V + D_KV
OFF_GB = OFF_GA + D_MODEL
D_IN_PROJ = OFF_GB + D_MODEL
LOG2E = 1.4426950408889634
ROPE_SPAN = 256

LANES = 128
BF16_ROWS = 16
F32_ROWS = 8
VMEM_LIMIT_BYTES = 60000 * 1024
IN_VMEM_LIMIT_BYTES = 57 * 1024 * 1024
MIX_VMEM_LIMIT_BYTES = 52 * 1024 * 1024

IN_TM = 512
IN_TN = 4352
MIX_TQ = 256
MIX_OUT_COLS = 256
FFN_TM = 1024
FFN_TF = 512
FFN_SUB = 256
FFN_ACT_ROWS = 128
FFN_UP_SPLIT = 2
ROW_CHUNK = 128
COL_CHUNK = 512

BF16 = jnp.bfloat16
F32 = jnp.float32


def _rmsnorm(x, g):
    ms = jnp.mean(x * x, axis=-1, keepdims=True)
    return x * lax.rsqrt(ms + EPS) * g


def _in_proj_kernel(x_ref, g_ref, w_ref, *rest, n_cast):
    src_refs = rest[:n_cast]
    z_ref = rest[n_cast]
    dst_refs = rest[n_cast + 1:2 * n_cast + 1]
    u_ref = rest[2 * n_cast + 1]

    @pl.when(pl.program_id(1) == 0)
    def _():
        def body(r, carry):
            rows = pl.ds(pl.multiple_of(r * ROW_CHUNK, ROW_CHUNK), ROW_CHUNK)
            u_ref[rows, :] = _rmsnorm(x_ref[rows, :], g_ref[...]).astype(BF16)
            return carry
        lax.fori_loop(0, x_ref.shape[0] // ROW_CHUNK, body, 0)

    for src, dst in zip(src_refs, dst_refs):
        dst[...] = src[...].astype(dst.dtype)

    z_ref[...] = jnp.dot(u_ref[...], w_ref[...],
                         preferred_element_type=F32).astype(z_ref.dtype)


def _slab_spec(shape, n_steps, step_fn):
    rows, cols = shape
    per = rows // n_steps
    hold = 1
    while per % BF16_ROWS:
        per, hold = per * 2, hold * 2
    assert per * (n_steps // hold) == rows
    return pl.BlockSpec((per, cols), lambda *idx: (step_fn(*idx) // hold, 0))


def _in_proj(x, g, w, cast_weights):
    s, d = x.shape
    n = w.shape[1]
    grid = (s // IN_TM, n // IN_TN)
    slabs = [_slab_spec(cw.shape, grid[0] * grid[1], lambda i, j: i * grid[1] + j)
             for cw in cast_weights]
    outs = pl.pallas_call(
        functools.partial(_in_proj_kernel, n_cast=len(cast_weights)),
        grid=grid,
        in_specs=[
            pl.BlockSpec((IN_TM, d), lambda i, j: (i, 0)),
            pl.BlockSpec((1, d), lambda i, j: (0, 0)),
            pl.BlockSpec((d, IN_TN), lambda i, j: (0, j)),
        ] + slabs,
        out_specs=[pl.BlockSpec((IN_TM, IN_TN), lambda i, j: (i, j))] + slabs,
        out_shape=[jax.ShapeDtypeStruct((s, n), BF16)]
        + [jax.ShapeDtypeStruct(cw.shape, BF16) for cw in cast_weights],
        scratch_shapes=[pltpu.VMEM((IN_TM, d), BF16)],
        compiler_params=pltpu.CompilerParams(
            dimension_semantics=("arbitrary", "arbitrary"),
            vmem_limit_bytes=IN_VMEM_LIMIT_BYTES),
        name="in_proj",
    )(x, g, w, *cast_weights)
    return outs[0], outs[1:]


def _rope(x, cos, sin_signed, first_half):
    partner = jnp.where(first_half, pltpu.roll(x, LANES - HEAD_DIM // 2, 1),
                        pltpu.roll(x, HEAD_DIM // 2, 1))
    return x * cos + partner * sin_signed


def _mixer_kernel(sink_ref, z_ref, kv_ref, kvp_ref, kvn_ref, cp_ref, vp_ref, cn_ref, vn_ref,
                  cos_ref, sin_ref, cosp_ref, sinp_ref, cosn_ref, sinn_ref, cw_ref, x_ref,
                  *rest, seq_len, n_cast):
    n_col = x_ref.shape[1] // COL_CHUNK
    gla_refs, glb_refs = rest[:n_col], rest[n_col:2 * n_col]
    bg_ref, woa_ref, wo_ref, wmix_ref = rest[2 * n_col:2 * n_col + 4]
    rest = rest[2 * n_col + 4:]
    src_refs = rest[:n_cast]
    h1_ref = rest[n_cast]
    dst_refs = rest[n_cast + 1:2 * n_cast + 1]
    (att_s_ref, ya_s_ref, qst_ref, kz_ref, vt_ref, cvs_ref,
     ys_a_ref, ys_b_ref, mix_ref) = rest[2 * n_cast + 1:]
    for src, dst in zip(src_refs, dst_refs):
        dst[...] = src[...].astype(dst.dtype)

    step = pl.program_id(0)
    nt = pl.num_programs(0) - 1
    i = jnp.minimum(step, nt - 1)
    slot = step % 2
    pslot = 1 - slot
    tq = z_ref.shape[0]
    d = x_ref.shape[1]

    @pl.when(step == 0)
    def _():
        att_s_ref[1] = jnp.zeros(att_s_ref.shape[1:], BF16)
        ya_s_ref[1] = jnp.zeros(ya_s_ref.shape[1:], BF16)

    def merge_stage1(c):
        cols = slice(c * COL_CHUNK, (c + 1) * COL_CHUNK)
        gcols = slice(d + c * COL_CHUNK, d + (c + 1) * COL_CHUNK)
        ys_a_ref[c % 2] = jnp.dot(ya_s_ref[pslot], woa_ref[:, cols], preferred_element_type=F32)
        ys_b_ref[c % 2] = jnp.dot(att_s_ref[pslot], wo_ref[:, cols], preferred_element_type=F32)
        for r0 in range(0, tq, ROW_CHUNK):
            rows = slice(r0, r0 + ROW_CHUNK)
            g_a = jax.nn.sigmoid(gla_refs[c][rows, :].astype(F32) + bg_ref[:, cols])
            g_b = jax.nn.sigmoid(glb_refs[c][rows, :].astype(F32) + bg_ref[:, gcols])
            mix_ref[rows, cols] = (g_a * ys_a_ref[c % 2, rows, :]
                                   + g_b * ys_b_ref[c % 2, rows, :]).astype(BF16)

    def merge_stage2(c):
        cols = slice(c * MIX_OUT_COLS, (c + 1) * MIX_OUT_COLS)
        h1_ref[:, cols] = x_ref[:, cols] + jnp.dot(mix_ref[...], wmix_ref[:, cols],
                                                   preferred_element_type=F32)

    for c in range(n_col):
        merge_stage1(c)
    merge_pieces = [functools.partial(merge_stage2, c) for c in range(d // MIX_OUT_COLS)]

    nb = tq // BLOCK
    grp = N_HEADS // N_KV_HEADS
    lane = lax.broadcasted_iota(jnp.int32, (1, LANES), 1)
    first_half = (lane % HEAD_DIM) < (HEAD_DIM // 2)
    low_head = lane < HEAD_DIM
    scale = HEAD_DIM ** -0.5 * LOG2E

    for b in range(nb):
        rows = slice(b * BLOCK, (b + 1) * BLOCK)
        cos_b = cos_ref[rows, :]
        sin_b = sin_ref[rows, :]
        for g in range(D_ATTN // LANES):
            h = g // 2
            xq = z_ref[rows, OFF_Q + g * LANES:OFF_Q + (g + 1) * LANES].astype(F32)
            xr = _rope(xq, cos_b, sin_b, first_half) * scale
            rolled = pltpu.roll(xr, HEAD_DIM, 1)
            even, odd = (xr, rolled) if h % 2 == 0 else (rolled, xr)
            j0 = 2 * (g % 2)
            qst_ref[h, b, j0 * BLOCK:(j0 + 1) * BLOCK, :] = even.astype(BF16)
            qst_ref[h, b, (j0 + 1) * BLOCK:(j0 + 2) * BLOCK, :] = odd.astype(BF16)

    sub = lax.broadcasted_iota(jnp.int32, (BF16_ROWS, BLOCK), 0)
    ones_rows = jnp.where(sub == 0, 1.0, 0.0).astype(F32)

    def put_kv(kv_blk_ref, c_ref, s_ref, blk, row0):
        src = slice(blk * BLOCK, (blk + 1) * BLOCK)
        dst = slice(row0, row0 + BLOCK)
        c = c_ref[src, :]
        s = s_ref[src, :]
        for pr in range(N_KV_HEADS // 2):
            kf = _rope(kv_blk_ref[src, pr * LANES:(pr + 1) * LANES].astype(F32), c, s, first_half)
            zero = jnp.zeros_like(kf)
            kz_ref[2 * pr, dst, :] = jnp.where(low_head, kf, zero).astype(BF16)
            kz_ref[2 * pr + 1, dst, :] = jnp.where(low_head, zero, kf).astype(BF16)
            vb_t = kv_blk_ref[src, D_KV + pr * LANES:D_KV + (pr + 1) * LANES].astype(F32).T
            vt_ref[2 * pr, :, dst] = jnp.concatenate(
                [vb_t[:HEAD_DIM], ones_rows], axis=0).astype(BF16)
            vt_ref[2 * pr + 1, :, dst] = jnp.concatenate(
                [vb_t[HEAD_DIM:], ones_rows], axis=0).astype(BF16)

    put_kv(kvp_ref, cosp_ref, sinp_ref, 0, 0)
    for blk in range(nb):
        put_kv(kv_ref, cos_ref, sin_ref, blk, (blk + 1) * BLOCK)
    put_kv(kvn_ref, cosn_ref, sinn_ref, 0, BLOCK + tq)

    c_io = lax.broadcasted_iota(jnp.int32, (BLOCK, BLOCK), 0)
    r_io = lax.broadcasted_iota(jnp.int32, (BLOCK, BLOCK), 1)

    def scores(b, h):
        return lax.dot_general(kz_ref[h, b * BLOCK:(b + 3) * BLOCK, :], qst_ref[h, b],
                               (((1,), (1,)), ((), ())),
                               preferred_element_type=F32)

    def finish(b, h, s_t):
        base = i * tq + b * BLOCK
        mask_lo = (c_io >= r_io) & (c_io + (base - BLOCK) >= 0)
        mask_hi = (c_io <= r_io) & (c_io + (base + BLOCK) < seq_len)
        probs, sink_terms = [], []
        for j in range(grp):
            sink = sink_ref[grp * h + j] * LOG2E
            cols = slice(j * BLOCK, (j + 1) * BLOCK)
            lo = jnp.where(mask_lo, s_t[0:BLOCK, cols], -jnp.inf)
            mid = s_t[BLOCK:2 * BLOCK, cols]
            hi = jnp.where(mask_hi, s_t[2 * BLOCK:3 * BLOCK, cols], -jnp.inf)
            m = jnp.maximum(jnp.maximum(jnp.max(lo, axis=0, keepdims=True),
                                        jnp.max(hi, axis=0, keepdims=True)),
                            jnp.maximum(jnp.max(mid, axis=0, keepdims=True), sink))
            probs.append(jnp.concatenate(
                [jnp.exp2(lo - m), jnp.exp2(mid - m), jnp.exp2(hi - m)], axis=0).astype(BF16))
            sink_terms.append(jnp.exp2(sink - m))
        p_t = jnp.concatenate(probs, axis=1)
        o_t = jnp.dot(vt_ref[h, :, b * BLOCK:(b + 3) * BLOCK], p_t,
                      preferred_element_type=F32)
        outs = []
        for j in range(grp):
            cols = slice(j * BLOCK, (j + 1) * BLOCK)
            denom = o_t[HEAD_DIM:HEAD_DIM + 1, cols] + sink_terms[j]
            outs.append(o_t[:HEAD_DIM, cols] * (1.0 / denom))
        for pr in range(grp // 2):
            pair_t = jnp.concatenate([outs[2 * pr], outs[2 * pr + 1]], axis=0)
            g = (grp // 2) * h + pr
            att_s_ref[slot, b * BLOCK:(b + 1) * BLOCK, g * LANES:(g + 1) * LANES] = (
                pair_t.T.astype(BF16))

    units = [(b, h) for b in range(nb) for h in range(N_KV_HEADS)]
    n_units, n_pieces = len(units), len(merge_pieces)
    s_next = scores(*units[0])
    for u, (b, h) in enumerate(units):
        s_t = s_next
        if u + 1 < n_units:
            s_next = scores(*units[u + 1])
        for piece in merge_pieces[u * n_pieces // n_units:(u + 1) * n_pieces // n_units]:
            piece()
        finish(b, h, s_t)

    zero8 = jnp.zeros((F32_ROWS, LANES), F32)
    for g in range(D_CONV // LANES):
        cols = slice(g * LANES, (g + 1) * LANES)
        zc = slice(OFF_C + g * LANES, OFF_C + (g + 1) * LANES)
        zv = slice(OFF_VA + g * LANES, OFF_VA + (g + 1) * LANES)
        zb = slice(OFF_B + g * LANES, OFF_B + (g + 1) * LANES)
        prev = (cp_ref[:, cols].astype(F32) * vp_ref[:, cols].astype(F32))[F32_ROWS:, :]
        nxt = (cn_ref[:, cols].astype(F32) * vn_ref[:, cols].astype(F32))[:F32_ROWS, :]
        cvs_ref[0:F32_ROWS, cols] = jnp.where(i > 0, prev, zero8)
        for r0 in range(0, tq, ROW_CHUNK):
            rows = slice(r0, r0 + ROW_CHUNK)
            cvs_ref[F32_ROWS + r0:F32_ROWS + r0 + ROW_CHUNK, cols] = (
                z_ref[rows, zc].astype(F32) * z_ref[rows, zv].astype(F32))
        cvs_ref[F32_ROWS + tq:2 * F32_ROWS + tq, cols] = jnp.where(i < nt - 1, nxt, zero8)
        for r0 in range(0, tq, ROW_CHUNK):
            lo = F32_ROWS + r0
            conv = (cvs_ref[lo - 1:lo - 1 + ROW_CHUNK, cols] * cw_ref[0:1, cols]
                    + cvs_ref[lo:lo + ROW_CHUNK, cols] * cw_ref[1:2, cols]
                    + cvs_ref[lo + 1:lo + 1 + ROW_CHUNK, cols] * cw_ref[2:3, cols])
            ya_s_ref[slot, r0:r0 + ROW_CHUNK, cols] = (
                z_ref[r0:r0 + ROW_CHUNK, zb].astype(F32) * conv).astype(BF16)


def _mixer(x, z, cos_t, sin_t, sink, conv_w, b_gate, w_out_a, w_o, w_mix, cast_weights):
    s, d = x.shape
    tq = MIX_TQ
    nt = s // tq
    kvb = tq // BLOCK
    cb = tq // BF16_ROWS
    qbcv = 4 * D_CONV
    kv_w = 2 * D_KV
    n_col = d // COL_CHUNK
    assert OFF_B == 0 and OFF_Q + D_ATTN == qbcv and OFF_K % kv_w == 0 and OFF_V == OFF_K + D_KV
    assert OFF_GA % COL_CHUNK == 0 and OFF_GB % COL_CHUNK == 0

    def br(step):
        return jnp.minimum(step, nt - 1)

    def mg(step):
        return jnp.maximum(step - 1, 0)

    def prev_blk(step, per):
        return jnp.maximum(br(step) * per - 1, 0)

    def next_blk(step, per, total):
        return jnp.minimum((br(step) + 1) * per, total - 1)

    def gate_spec(off, c):
        return pl.BlockSpec((tq, COL_CHUNK), lambda i: (mg(i), off // COL_CHUNK + c))

    const = lambda i: (0, 0)
    single = dict(pipeline_mode=pl.Buffered(1))
    in_specs = [
        pl.BlockSpec(memory_space=pltpu.SMEM),
        pl.BlockSpec((tq, qbcv), lambda i: (br(i), 0)),
        pl.BlockSpec((tq, kv_w), lambda i: (br(i), OFF_K // kv_w)),
        pl.BlockSpec((BLOCK, kv_w), lambda i: (prev_blk(i, kvb), OFF_K // kv_w)),
        pl.BlockSpec((BLOCK, kv_w), lambda i: (next_blk(i, kvb, s // BLOCK), OFF_K // kv_w)),
        pl.BlockSpec((BF16_ROWS, D_CONV), lambda i: (prev_blk(i, cb), OFF_C // D_CONV)),
        pl.BlockSpec((BF16_ROWS, D_CONV), lambda i: (prev_blk(i, cb), OFF_VA // D_CONV)),
        pl.BlockSpec((BF16_ROWS, D_CONV), lambda i: (next_blk(i, cb, s // BF16_ROWS), OFF_C // D_CONV)),
        pl.BlockSpec((BF16_ROWS, D_CONV), lambda i: (next_blk(i, cb, s // BF16_ROWS), OFF_VA // D_CONV)),
        pl.BlockSpec((tq, LANES), lambda i: (br(i), 0)),
        pl.BlockSpec((tq, LANES), lambda i: (br(i), 0)),
        pl.BlockSpec((BLOCK, LANES), lambda i: (prev_blk(i, kvb), 0)),
        pl.BlockSpec((BLOCK, LANES), lambda i: (prev_blk(i, kvb), 0)),
        pl.BlockSpec((BLOCK, LANES), lambda i: (next_blk(i, kvb, s // BLOCK), 0)),
        pl.BlockSpec((BLOCK, LANES), lambda i: (next_blk(i, kvb, s // BLOCK), 0)),
        pl.BlockSpec((3, D_CONV), const),
        pl.BlockSpec((tq, d), lambda i: (mg(i), 0)),
    ] + [gate_spec(OFF_GA, c) for c in range(n_col)] + [
        gate_spec(OFF_GB, c) for c in range(n_col)] + [
        pl.BlockSpec((1, 2 * d), const),
        pl.BlockSpec((D_CONV, d), const, **single),
        pl.BlockSpec((D_ATTN, d), const, **single),
        pl.BlockSpec((d, d), const, **single),
    ]
    r_ext = tq + 2 * BLOCK
    grp = N_HEADS // N_KV_HEADS
    slabs = [_slab_spec(cw.shape, nt, br) for cw in cast_weights]
    outs = pl.pallas_call(
        functools.partial(_mixer_kernel, seq_len=s, n_cast=len(cast_weights)),
        grid=(nt + 1,),
        in_specs=in_specs + slabs,
        out_specs=[pl.BlockSpec((tq, d), lambda i: (mg(i), 0))] + slabs,
        out_shape=[jax.ShapeDtypeStruct((s, d), F32)]
        + [jax.ShapeDtypeStruct(cw.shape, BF16) for cw in cast_weights],
        scratch_shapes=[
            pltpu.VMEM((2, tq, D_ATTN), BF16),
            pltpu.VMEM((2, tq, D_CONV), BF16),
            pltpu.VMEM((N_KV_HEADS, tq // BLOCK, grp * BLOCK, LANES), BF16),
            pltpu.VMEM((N_KV_HEADS, r_ext, LANES), BF16),
            pltpu.VMEM((N_KV_HEADS, HEAD_DIM + BF16_ROWS, r_ext), BF16),
            pltpu.VMEM((tq + 2 * F32_ROWS, D_CONV), F32),
            pltpu.VMEM((2, tq, COL_CHUNK), F32),
            pltpu.VMEM((2, tq, COL_CHUNK), F32),
            pltpu.VMEM((tq, d), BF16),
        ],
        compiler_params=pltpu.CompilerParams(
            dimension_semantics=("arbitrary",),
            vmem_limit_bytes=MIX_VMEM_LIMIT_BYTES),
        name="mixer",
    )(sink, z, z, z, z, z, z, z, z, cos_t, sin_t, cos_t, sin_t, cos_t, sin_t, conv_w, x,
      *([z] * (2 * n_col)), b_gate, w_out_a, w_o, w_mix, *cast_weights)
    return outs[0], outs[1:]


def _ffn_kernel(h_ref, hp_ref, hn_ref, g2_ref, wa_ref, wg_ref, cwa_ref, cwg_ref,
                ba_ref, bgt_ref, wd_ref, gf_ref,
                o_ref,
                u_ref, *sub_refs, final_norm):
    i = pl.program_id(0)
    j = pl.program_id(1)
    nt = pl.num_programs(0)
    nf = pl.num_programs(1)
    tm = h_ref.shape[0]
    halo = BF16_ROWS
    n_sub = len(sub_refs) // 3
    sa_refs, sg_refs, act_refs = sub_refs[0::3], sub_refs[1::3], sub_refs[2::3]

    @pl.when(j == 0)
    def _():
        g2 = g2_ref[...]
        zero = jnp.zeros((halo, h_ref.shape[1]), F32)
        prev_rows = jnp.where(i > 0, _rmsnorm(hp_ref[...], g2), zero)
        next_rows = jnp.where(i < nt - 1, _rmsnorm(hn_ref[...], g2), zero)
        first_row = lax.broadcasted_iota(jnp.int32, (halo, 1), 0) == 0
        u_ref[0:halo, :] = jnp.where(first_row, next_rows, prev_rows).astype(BF16)

        def body(r, carry):
            r0 = pl.multiple_of(r * ROW_CHUNK, ROW_CHUNK)
            u_ref[pl.ds(halo + r0, ROW_CHUNK), :] = _rmsnorm(
                h_ref[pl.ds(r0, ROW_CHUNK), :], g2).astype(BF16)
            o_ref[pl.ds(r0, ROW_CHUNK), :] = jnp.zeros((ROW_CHUNK, o_ref.shape[1]), F32)
            return carry
        lax.fori_loop(0, tm // ROW_CHUNK, body, 0)

    def conv(s_ref, cw_ref, b_ref, r0, cs):
        return (s_ref[r0 + halo - 1:r0 + halo - 1 + FFN_ACT_ROWS, :] * cw_ref[0:1, cs]
                + s_ref[r0 + halo:r0 + halo + FFN_ACT_ROWS, :] * cw_ref[1:2, cs]
                + s_ref[r0 + halo + 1:r0 + halo + 1 + FFN_ACT_ROWS, :] * cw_ref[2:3, cs]) + b_ref[:, cs]

    def sub_cols(k):
        return slice(k * FFN_SUB, (k + 1) * FFN_SUB)

    def up(k, w_ref, s_refs):
        rows_all = halo + tm
        blk = -(-rows_all // (FFN_UP_SPLIT * BF16_ROWS)) * BF16_ROWS
        for r0 in range(0, rows_all, blk):
            rows = slice(r0, min(r0 + blk, rows_all))
            s_refs[k][rows, :] = jnp.dot(u_ref[rows, :], w_ref[:, sub_cols(k)],
                                         preferred_element_type=F32)
        s_refs[k][rows_all:rows_all + F32_ROWS, :] = s_refs[k][0:F32_ROWS, :]

    def activate(k):
        for r0 in range(0, tm, FFN_ACT_ROWS):
            a = conv(sa_refs[k], cwa_ref, ba_ref, r0, sub_cols(k))
            g = conv(sg_refs[k], cwg_ref, bgt_ref, r0, sub_cols(k))
            act_refs[k][r0:r0 + FFN_ACT_ROWS, :] = (a * jax.nn.sigmoid(a) * g).astype(BF16)

    def down(k):
        for c in range(o_ref.shape[1] // COL_CHUNK):
            cols = slice(c * COL_CHUNK, (c + 1) * COL_CHUNK)
            o_ref[:, cols] += jnp.dot(act_refs[k][...], wd_ref[sub_cols(k), cols],
                                      preferred_element_type=F32)

    for k in range(n_sub):
        up(k, wa_ref, sa_refs)
        up(k, wg_ref, sg_refs)
    for k in range(n_sub):
        activate(k)
        down(k)

    @pl.when(j == nf - 1)
    def _():
        def body(r, carry):
            rows = pl.ds(pl.multiple_of(r * ROW_CHUNK, ROW_CHUNK), ROW_CHUNK)
            h = h_ref[rows, :] + o_ref[rows, :]
            if final_norm:
                h = _rmsnorm(h, gf_ref[...])
            o_ref[rows, :] = h
            return carry
        lax.fori_loop(0, tm // ROW_CHUNK, body, 0)


def _ffn(h, g2, w_up, conv_w, conv_b, w_down, gf, final_norm):
    s, d = h.shape
    tm, tf = FFN_TM, FFN_TF
    nt, nf = s // tm, D_FF // tf
    hb = tm // BF16_ROWS
    n_sub = tf // FFN_SUB

    in_specs = [
        pl.BlockSpec((tm, d), lambda i, j: (i, 0)),
        pl.BlockSpec((BF16_ROWS, d), lambda i, j: (jnp.maximum(i * hb - 1, 0), 0)),
        pl.BlockSpec((BF16_ROWS, d), lambda i, j: (jnp.minimum((i + 1) * hb, s // BF16_ROWS - 1), 0)),
        pl.BlockSpec((1, d), lambda i, j: (0, 0)),
        pl.BlockSpec((d, tf), lambda i, j: (0, j)),
        pl.BlockSpec((d, tf), lambda i, j: (0, nf + j)),
        pl.BlockSpec((3, tf), lambda i, j: (0, j)),
        pl.BlockSpec((3, tf), lambda i, j: (0, nf + j)),
        pl.BlockSpec((1, tf), lambda i, j: (0, j)),
        pl.BlockSpec((1, tf), lambda i, j: (0, nf + j)),
        pl.BlockSpec((tf, d), lambda i, j: (j, 0)),
        pl.BlockSpec((1, d), lambda i, j: (0, 0)),
    ]
    return pl.pallas_call(
        functools.partial(_ffn_kernel, final_norm=final_norm),
        grid=(nt, nf),
        in_specs=in_specs,
        out_specs=pl.BlockSpec((tm, d), lambda i, j: (i, 0)),
        out_shape=jax.ShapeDtypeStruct((s, d), F32),
        scratch_shapes=[
            pltpu.VMEM((BF16_ROWS + tm, d), BF16),
        ] + [
            pltpu.VMEM((tm + 2 * BF16_ROWS, FFN_SUB), F32),
            pltpu.VMEM((tm + 2 * BF16_ROWS, FFN_SUB), F32),
            pltpu.VMEM((tm, FFN_SUB), BF16),
        ] * n_sub,
        compiler_params=pltpu.CompilerParams(
            dimension_semantics=("arbitrary", "arbitrary"),
            vmem_limit_bytes=VMEM_LIMIT_BYTES),
        name="ffn",
    )(h, h, h, g2, w_up, w_up, conv_w, conv_w, conv_b, conv_b, w_down, gf)


def _rope_tables(seq_len):
    half = HEAD_DIM // 2
    reps = LANES // half
    inv_freq = jnp.tile(ROPE_THETA ** (-jnp.arange(0, half, dtype=F32) / half), reps)[None, :]
    sign = jnp.tile(jnp.concatenate([-jnp.ones((half,), F32), jnp.ones((half,), F32)]),
                    reps // 2)
    base = (ROPE_SPAN * jnp.arange(seq_len // ROPE_SPAN, dtype=F32))[:, None] * inv_freq
    offs = jnp.arange(ROPE_SPAN, dtype=F32)[:, None] * inv_freq
    cb, sb = jnp.cos(base)[:, None, :], jnp.sin(base)[:, None, :]
    co, so = jnp.cos(offs)[None, :, :], jnp.sin(offs)[None, :, :]
    cos_t = (cb * co - sb * so).reshape(seq_len, LANES)
    sin_t = ((sb * co + cb * so) * sign).reshape(seq_len, LANES)
    return cos_t, sin_t


def kernel(x, norm_mix_g, w_in, b_gate, conv_a_w, w_out_a, sink_logits, w_o_attn, w_mix_out,
           norm_ffn_g, ffn_w_up, ffn_conv_w, ffn_conv_b, ffn_w_down, norm_final_g):
    b, s, d = x.shape
    depth = w_in.shape[0]
    cos_t, sin_t = _rope_tables(s)
    outs = []
    for bi in range(b):
        h = x[bi]
        for l in range(depth):
            z, (w_oa, w_o, w_mix) = _in_proj(
                h, norm_mix_g[l][None, :], w_in[l].astype(BF16),
                (w_out_a[l], w_o_attn[l], w_mix_out[l]))
            h, (w_up, w_down) = _mixer(
                h, z, cos_t, sin_t, sink_logits[l], conv_a_w[l], b_gate[l][None, :],
                w_oa, w_o, w_mix, (ffn_w_up[l], ffn_w_down[l]))
            h = _ffn(h, norm_ffn_g[l][None, :], w_up, ffn_conv_w[l], ffn_conv_b[l][None, :],
                     w_down, norm_final_g[None, :], final_norm=(l == depth - 1))
        outs.append(h[None])
    return outs[0] if b == 1 else jnp.concatenate(outs, axis=0)
```

```python
import functools

import jax
import jax.numpy as jnp
from jax import lax
from jax.experimental import pallas as pl
from jax.experimental.pallas import tpu as pltpu

D_MODEL = 2048
D_CONV = D_MODEL // 2
N_HEADS = 16
N_KV_HEADS = 4
HEAD_DIM = 64
D_ATTN = N_HEADS * HEAD_DIM
D_KV = N_KV_HEADS * HEAD_DIM
WINDOW = 128
BLOCK = 128
ROPE_THETA = 10000.0
D_FF = 5632
EPS = 1e-6

OFF_B = 0
OFF_C = OFF_B + D_CONV
OFF_VA = OFF_C + D_CONV
OFF_Q = OFF_VA + D_CONV
OFF_K = OFF_Q + D_ATTN
OFF_V = OFF_K + D_KV
OFF_GA = OFF_V + D_KV
OFF_GB = OFF_GA + D_MODEL
D_IN_PROJ = OFF_GB + D_MODEL
LOG2E = 1.4426950408889634
ROPE_SPAN = 256

LANES = 128
BF16_ROWS = 16
F32_ROWS = 8
VMEM_LIMIT_BYTES = 60000 * 1024
IN_VMEM_LIMIT_BYTES = 57 * 1024 * 1024
MIX_VMEM_LIMIT_BYTES = 52 * 1024 * 1024

IN_TM = 512
IN_TN = 4352
IN_NORM_ROWS = 256
MIX_TQ = 256
MIX_OUT_COLS = 256
FFN_TM = 1024
FFN_TF = 512
FFN_SUB = 256
FFN_ACT_ROWS = 128
FFN_UP_SPLIT = 2
ROW_CHUNK = 128
COL_CHUNK = 512

BF16 = jnp.bfloat16
F32 = jnp.float32


def _rmsnorm(x, g):
    ms = jnp.mean(x * x, axis=-1, keepdims=True)
    return x * lax.rsqrt(ms + EPS) * g


def _in_proj_kernel(x_ref, g_ref, w_ref, *rest, n_cast):
    src_refs = rest[:n_cast]
    z_ref = rest[n_cast]
    dst_refs = rest[n_cast + 1:2 * n_cast + 1]
    u_ref = rest[2 * n_cast + 1]

    for src, dst in zip(src_refs, dst_refs):
        dst[...] = src[...].astype(dst.dtype)

    @pl.when(pl.program_id(1) == 0)
    def _():
        for r0 in range(0, x_ref.shape[0], IN_NORM_ROWS):
            for r in range(r0, r0 + IN_NORM_ROWS, ROW_CHUNK):
                rows = slice(r, r + ROW_CHUNK)
                u_ref[rows, :] = _rmsnorm(x_ref[rows, :], g_ref[...]).astype(BF16)
            blk = slice(r0, r0 + IN_NORM_ROWS)
            z_ref[blk, :] = jnp.dot(u_ref[blk, :], w_ref[...],
                                    preferred_element_type=F32).astype(z_ref.dtype)

    @pl.when(pl.program_id(1) > 0)
    def _():
        z_ref[...] = jnp.dot(u_ref[...], w_ref[...],
                             preferred_element_type=F32).astype(z_ref.dtype)


def _slab_spec(shape, n_steps, step_fn):
    rows, cols = shape
    per = rows // n_steps
    hold = 1
    while per % BF16_ROWS:
        per, hold = per * 2, hold * 2
    assert per * (n_steps // hold) == rows
    return pl.BlockSpec((per, cols), lambda *idx: (step_fn(*idx) // hold, 0))


def _in_proj(x, g, w, cast_weights):
    s, d = x.shape
    n = w.shape[1]
    grid = (s // IN_TM, n // IN_TN)
    slabs = [_slab_spec(cw.shape, grid[0] * grid[1], lambda i, j: i * grid[1] + j)
             for cw in cast_weights]
    outs = pl.pallas_call(
        functools.partial(_in_proj_kernel, n_cast=len(cast_weights)),
        grid=grid,
        in_specs=[
            pl.BlockSpec((IN_TM, d), lambda i, j: (i, 0)),
            pl.BlockSpec((1, d), lambda i, j: (0, 0)),
            pl.BlockSpec((d, IN_TN), lambda i, j: (0, j)),
        ] + slabs,
        out_specs=[pl.BlockSpec((IN_TM, IN_TN), lambda i, j: (i, j))] + slabs,
        out_shape=[jax.ShapeDtypeStruct((s, n), BF16)]
        + [jax.ShapeDtypeStruct(cw.shape, BF16) for cw in cast_weights],
        scratch_shapes=[pltpu.VMEM((IN_TM, d), BF16)],
        compiler_params=pltpu.CompilerParams(
            dimension_semantics=("arbitrary", "arbitrary"),
            vmem_limit_bytes=IN_VMEM_LIMIT_BYTES),
        name="in_proj",
    )(x, g, w, *cast_weights)
    return outs[0], outs[1:]


def _rope(x, cos, sin_signed, first_half):
    partner = jnp.where(first_half, pltpu.roll(x, LANES - HEAD_DIM // 2, 1),
                        pltpu.roll(x, HEAD_DIM // 2, 1))
    return x * cos + partner * sin_signed


def _mixer_kernel(sink_ref, z_ref, kv_ref, kvp_ref, kvn_ref, cp_ref, vp_ref, cn_ref, vn_ref,
                  cos_ref, sin_ref, cosp_ref, sinp_ref, cosn_ref, sinn_ref, cw_ref, x_ref,
                  *rest, seq_len, n_cast):
    n_col = x_ref.shape[1] // COL_CHUNK
    gla_refs, glb_refs = rest[:n_col], rest[n_col:2 * n_col]
    bg_ref, woa_ref, wo_ref, wmix_ref = rest[2 * n_col:2 * n_col + 4]
    rest = rest[2 * n_col + 4:]
    src_refs = rest[:n_cast]
    h1_ref = rest[n_cast]
    dst_refs = rest[n_cast + 1:2 * n_cast + 1]
    (att_s_ref, ya_s_ref, qst_ref, kz_ref, vt_ref, cvs_ref,
     ys_a_ref, ys_b_ref, mix_ref) = rest[2 * n_cast + 1:]
    for src, dst in zip(src_refs, dst_refs):
        dst[...] = src[...].astype(dst.dtype)

    step = pl.program_id(0)
    nt = pl.num_programs(0) - 1
    i = jnp.minimum(step, nt - 1)
    slot = step % 2
    pslot = 1 - slot
    tq = z_ref.shape[0]
    d = x_ref.shape[1]

    @pl.when(step == 0)
    def _():
        att_s_ref[1] = jnp.zeros(att_s_ref.shape[1:], BF16)
        ya_s_ref[1] = jnp.zeros(ya_s_ref.shape[1:], BF16)

    def merge_stage1(c):
        cols = slice(c * COL_CHUNK, (c + 1) * COL_CHUNK)
        gcols = slice(d + c * COL_CHUNK, d + (c + 1) * COL_CHUNK)
        ys_a_ref[c % 2] = jnp.dot(ya_s_ref[pslot], woa_ref[:, cols], preferred_element_type=F32)
        ys_b_ref[c % 2] = jnp.dot(att_s_ref[pslot], wo_ref[:, cols], preferred_element_type=F32)
        for r0 in range(0, tq, ROW_CHUNK):
            rows = slice(r0, r0 + ROW_CHUNK)
            g_a = jax.nn.sigmoid(gla_refs[c][rows, :].astype(F32) + bg_ref[:, cols])
            g_b = jax.nn.sigmoid(glb_refs[c][rows, :].astype(F32) + bg_ref[:, gcols])
            mix_ref[rows, cols] = (g_a * ys_a_ref[c % 2, rows, :]
                                   + g_b * ys_b_ref[c % 2, rows, :]).astype(BF16)

    def merge_stage2(c):
        cols = slice(c * MIX_OUT_COLS, (c + 1) * MIX_OUT_COLS)
        h1_ref[:, cols] = x_ref[:, cols] + jnp.dot(mix_ref[...], wmix_ref[:, cols],
                                                   preferred_element_type=F32)

    for c in range(n_col):
        merge_stage1(c)
    merge_pieces = [functools.partial(merge_stage2, c) for c in range(d // MIX_OUT_COLS)]

    nb = tq // BLOCK
    grp = N_HEADS // N_KV_HEADS
    lane = lax.broadcasted_iota(jnp.int32, (1, LANES), 1)
    first_half = (lane % HEAD_DIM) < (HEAD_DIM // 2)
    low_head = lane < HEAD_DIM
    scale = HEAD_DIM ** -0.5 * LOG2E

    for b in range(nb):
        rows = slice(b * BLOCK, (b + 1) * BLOCK)
        cos_b = cos_ref[rows, :]
        sin_b = sin_ref[rows, :]
        for g in range(D_ATTN // LANES):
            h = g // 2
            xq = z_ref[rows, OFF_Q + g * LANES:OFF_Q + (g + 1) * LANES].astype(F32)
            xr = _rope(xq, cos_b, sin_b, first_half) * scale
            rolled = pltpu.roll(xr, HEAD_DIM, 1)
            even, odd = (xr, rolled) if h % 2 == 0 else (rolled, xr)
            j0 = 2 * (g % 2)
            qst_ref[h, b, j0 * BLOCK:(j0 + 1) * BLOCK, :] = even.astype(BF16)
            qst_ref[h, b, (j0 + 1) * BLOCK:(j0 + 2) * BLOCK, :] = odd.astype(BF16)

    sub = lax.broadcasted_iota(jnp.int32, (BF16_ROWS, BLOCK), 0)
    ones_rows = jnp.where(sub == 0, 1.0, 0.0).astype(F32)

    def put_kv(kv_blk_ref, c_ref, s_ref, blk, row0):
        src = slice(blk * BLOCK, (blk + 1) * BLOCK)
        dst = slice(row0, row0 + BLOCK)
        c = c_ref[src, :]
        s = s_ref[src, :]
        for pr in range(N_KV_HEADS // 2):
            kf = _rope(kv_blk_ref[src, pr * LANES:(pr + 1) * LANES].astype(F32), c, s, first_half)
            zero = jnp.zeros_like(kf)
            kz_ref[2 * pr, dst, :] = jnp.where(low_head, kf, zero).astype(BF16)
            kz_ref[2 * pr + 1, dst, :] = jnp.where(low_head, zero, kf).astype(BF16)
            vb_t = kv_blk_ref[src, D_KV + pr * LANES:D_KV + (pr + 1) * LANES].astype(F32).T
            vt_ref[2 * pr, :, dst] = jnp.concatenate(
                [vb_t[:HEAD_DIM], ones_rows], axis=0).astype(BF16)
            vt_ref[2 * pr + 1, :, dst] = jnp.concatenate(
                [vb_t[HEAD_DIM:], ones_rows], axis=0).astype(BF16)

    put_kv(kvp_ref, cosp_ref, sinp_ref, 0, 0)
    for blk in range(nb):
        put_kv(kv_ref, cos_ref, sin_ref, blk, (blk + 1) * BLOCK)
    put_kv(kvn_ref, cosn_ref, sinn_ref, 0, BLOCK + tq)

    c_io = lax.broadcasted_iota(jnp.int32, (BLOCK, BLOCK), 0)
    r_io = lax.broadcasted_iota(jnp.int32, (BLOCK, BLOCK), 1)

    def scores(b, h):
        return lax.dot_general(kz_ref[h, b * BLOCK:(b + 3) * BLOCK, :], qst_ref[h, b],
                               (((1,), (1,)), ((), ())),
                               preferred_element_type=F32)

    def finish(b, h, s_t):
        base = i * tq + b * BLOCK
        mask_lo = (c_io >= r_io) & (c_io + (base - BLOCK) >= 0)
        mask_hi = (c_io <= r_io) & (c_io + (base + BLOCK) < seq_len)
        probs, sink_terms = [], []
        for j in range(grp):
            sink = sink_ref[grp * h + j] * LOG2E
            cols = slice(j * BLOCK, (j + 1) * BLOCK)
            lo = jnp.where(mask_lo, s_t[0:BLOCK, cols], -jnp.inf)
            mid = s_t[BLOCK:2 * BLOCK, cols]
            hi = jnp.where(mask_hi, s_t[2 * BLOCK:3 * BLOCK, cols], -jnp.inf)
            m = jnp.maximum(jnp.maximum(jnp.max(lo, axis=0, keepdims=True),
                                        jnp.max(hi, axis=0, keepdims=True)),
                            jnp.maximum(jnp.max(mid, axis=0, keepdims=True), sink))
            probs.append(jnp.concatenate(
                [jnp.exp2(lo - m), jnp.exp2(mid - m), jnp.exp2(hi - m)], axis=0).astype(BF16))
            sink_terms.append(jnp.exp2(sink - m))
        p_t = jnp.concatenate(probs, axis=1)
        o_t = jnp.dot(vt_ref[h, :, b * BLOCK:(b + 3) * BLOCK], p_t,
                      preferred_element_type=F32)
        outs = []
        for j in range(grp):
            cols = slice(j * BLOCK, (j + 1) * BLOCK)
            denom = o_t[HEAD_DIM:HEAD_DIM + 1, cols] + sink_terms[j]
            outs.append(o_t[:HEAD_DIM, cols] * (1.0 / denom))
        for pr in range(grp // 2):
            pair_t = jnp.concatenate([outs[2 * pr], outs[2 * pr + 1]], axis=0)
            g = (grp // 2) * h + pr
            att_s_ref[slot, b * BLOCK:(b + 1) * BLOCK, g * LANES:(g + 1) * LANES] = (
                pair_t.T.astype(BF16))

    units = [(b, h) for b in range(nb) for h in range(N_KV_HEADS)]
    n_units, n_pieces = len(units), len(merge_pieces)
    s_next = scores(*units[0])
    for u, (b, h) in enumerate(units):
        s_t = s_next
        if u + 1 < n_units:
            s_next = scores(*units[u + 1])
        for piece in merge_pieces[u * n_pieces // n_units:(u + 1) * n_pieces // n_units]:
            piece()
        finish(b, h, s_t)

    zero8 = jnp.zeros((F32_ROWS, LANES), F32)
    for g in range(D_CONV // LANES):
        cols = slice(g * LANES, (g + 1) * LANES)
        zc = slice(OFF_C + g * LANES, OFF_C + (g + 1) * LANES)
        zv = slice(OFF_VA + g * LANES, OFF_VA + (g + 1) * LANES)
        zb = slice(OFF_B + g * LANES, OFF_B + (g + 1) * LANES)
        prev = (cp_ref[:, cols].astype(F32) * vp_ref[:, cols].astype(F32))[F32_ROWS:, :]
        nxt = (cn_ref[:, cols].astype(F32) * vn_ref[:, cols].astype(F32))[:F32_ROWS, :]
        cvs_ref[0:F32_ROWS, cols] = jnp.where(i > 0, prev, zero8)
        for r0 in range(0, tq, ROW_CHUNK):
            rows = slice(r0, r0 + ROW_CHUNK)
            cvs_ref[F32_ROWS + r0:F32_ROWS + r0 + ROW_CHUNK, cols] = (
                z_ref[rows, zc].astype(F32) * z_ref[rows, zv].astype(F32))
        cvs_ref[F32_ROWS + tq:2 * F32_ROWS + tq, cols] = jnp.where(i < nt - 1, nxt, zero8)
        for r0 in range(0, tq, ROW_CHUNK):
            lo = F32_ROWS + r0
            conv = (cvs_ref[lo - 1:lo - 1 + ROW_CHUNK, cols] * cw_ref[0:1, cols]
                    + cvs_ref[lo:lo + ROW_CHUNK, cols] * cw_ref[1:2, cols]
                    + cvs_ref[lo + 1:lo + 1 + ROW_CHUNK, cols] * cw_ref[2:3, cols])
            ya_s_ref[slot, r0:r0 + ROW_CHUNK, cols] = (
                z_ref[r0:r0 + ROW_CHUNK, zb].astype(F32) * conv).astype(BF16)


def _mixer(x, z, cos_t, sin_t, sink, conv_w, b_gate, w_out_a, w_o, w_mix, cast_weights):
    s, d = x.shape
    tq = MIX_TQ
    nt = s // tq
    kvb = tq // BLOCK
    cb = tq // BF16_ROWS
    qbcv = 4 * D_CONV
    kv_w = 2 * D_KV
    n_col = d // COL_CHUNK
    assert OFF_B == 0 and OFF_Q + D_ATTN == qbcv and OFF_K % kv_w == 0 and OFF_V == OFF_K + D_KV
    assert OFF_GA % COL_CHUNK == 0 and OFF_GB % COL_CHUNK == 0

    def br(step):
        return jnp.minimum(step, nt - 1)

    def mg(step):
        return jnp.maximum(step - 1, 0)

    def prev_blk(step, per):
        return jnp.maximum(br(step) * per - 1, 0)

    def next_blk(step, per, total):
        return jnp.minimum((br(step) + 1) * per, total - 1)

    def gate_spec(off, c):
        return pl.BlockSpec((tq, COL_CHUNK), lambda i: (mg(i), off // COL_CHUNK + c))

    const = lambda i: (0, 0)
    single = dict(pipeline_mode=pl.Buffered(1))
    in_specs = [
        pl.BlockSpec(memory_space=pltpu.SMEM),
        pl.BlockSpec((tq, qbcv), lambda i: (br(i), 0)),
        pl.BlockSpec((tq, kv_w), lambda i: (br(i), OFF_K // kv_w)),
        pl.BlockSpec((BLOCK, kv_w), lambda i: (prev_blk(i, kvb), OFF_K // kv_w)),
        pl.BlockSpec((BLOCK, kv_w), lambda i: (next_blk(i, kvb, s // BLOCK), OFF_K // kv_w)),
        pl.BlockSpec((BF16_ROWS, D_CONV), lambda i: (prev_blk(i, cb), OFF_C // D_CONV)),
        pl.BlockSpec((BF16_ROWS, D_CONV), lambda i: (prev_blk(i, cb), OFF_VA // D_CONV)),
        pl.BlockSpec((BF16_ROWS, D_CONV), lambda i: (next_blk(i, cb, s // BF16_ROWS), OFF_C // D_CONV)),
        pl.BlockSpec((BF16_ROWS, D_CONV), lambda i: (next_blk(i, cb, s // BF16_ROWS), OFF_VA // D_CONV)),
        pl.BlockSpec((tq, LANES), lambda i: (br(i), 0)),
        pl.BlockSpec((tq, LANES), lambda i: (br(i), 0)),
        pl.BlockSpec((BLOCK, LANES), lambda i: (prev_blk(i, kvb), 0)),
        pl.BlockSpec((BLOCK, LANES), lambda i: (prev_blk(i, kvb), 0)),
        pl.BlockSpec((BLOCK, LANES), lambda i: (next_blk(i, kvb, s // BLOCK), 0)),
        pl.BlockSpec((BLOCK, LANES), lambda i: (next_blk(i, kvb, s // BLOCK), 0)),
        pl.BlockSpec((3, D_CONV), const),
        pl.BlockSpec((tq, d), lambda i: (mg(i), 0)),
    ] + [gate_spec(OFF_GA, c) for c in range(n_col)] + [
        gate_spec(OFF_GB, c) for c in range(n_col)] + [
        pl.BlockSpec((1, 2 * d), const),
        pl.BlockSpec((D_CONV, d), const, **single),
        pl.BlockSpec((D_ATTN, d), const, **single),
        pl.BlockSpec((d, d), const, **single),
    ]
    r_ext = tq + 2 * BLOCK
    grp = N_HEADS // N_KV_HEADS
    slabs = [_slab_spec(cw.shape, nt, br) for cw in cast_weights]
    outs = pl.pallas_call(
        functools.partial(_mixer_kernel, seq_len=s, n_cast=len(cast_weights)),
        grid=(nt + 1,),
        in_specs=in_specs + slabs,
        out_specs=[pl.BlockSpec((tq, d), lambda i: (mg(i), 0))] + slabs,
        out_shape=[jax.ShapeDtypeStruct((s, d), F32)]
        + [jax.ShapeDtypeStruct(cw.shape, BF16) for cw in cast_weights],
        scratch_shapes=[
            pltpu.VMEM((2, tq, D_ATTN), BF16),
            pltpu.VMEM((2, tq, D_CONV), BF16),
            pltpu.VMEM((N_KV_HEADS, tq // BLOCK, grp * BLOCK, LANES), BF16),
            pltpu.VMEM((N_KV_HEADS, r_ext, LANES), BF16),
            pltpu.VMEM((N_KV_HEADS, HEAD_DIM + BF16_ROWS, r_ext), BF16),
            pltpu.VMEM((tq + 2 * F32_ROWS, D_CONV), F32),
            pltpu.VMEM((2, tq, COL_CHUNK), F32),
            pltpu.VMEM((2, tq, COL_CHUNK), F32),
            pltpu.VMEM((tq, d), BF16),
        ],
        compiler_params=pltpu.CompilerParams(
            dimension_semantics=("arbitrary",),
            vmem_limit_bytes=MIX_VMEM_LIMIT_BYTES),
        name="mixer",
    )(sink, z, z, z, z, z, z, z, z, cos_t, sin_t, cos_t, sin_t, cos_t, sin_t, conv_w, x,
      *([z] * (2 * n_col)), b_gate, w_out_a, w_o, w_mix, *cast_weights)
    return outs[0], outs[1:]


def _ffn_kernel(h_ref, hp_ref, hn_ref, g2_ref, wa_ref, wg_ref, cwa_ref, cwg_ref,
                ba_ref, bgt_ref, wd_ref, gf_ref,
                o_ref,
                u_ref, *sub_refs, final_norm):
    i = pl.program_id(0)
    j = pl.program_id(1)
    nt = pl.num_programs(0)
    nf = pl.num_programs(1)
    tm = h_ref.shape[0]
    halo = BF16_ROWS
    n_sub = len(sub_refs) // 3
    sa_refs, sg_refs, act_refs = sub_refs[0::3], sub_refs[1::3], sub_refs[2::3]

    @pl.when(j == 0)
    def _():
        g2 = g2_ref[...]
        zero = jnp.zeros((halo, h_ref.shape[1]), F32)
        prev_rows = jnp.where(i > 0, _rmsnorm(hp_ref[...], g2), zero)
        next_rows = jnp.where(i < nt - 1, _rmsnorm(hn_ref[...], g2), zero)
        first_row = lax.broadcasted_iota(jnp.int32, (halo, 1), 0) == 0
        u_ref[0:halo, :] = jnp.where(first_row, next_rows, prev_rows).astype(BF16)

        def body(r, carry):
            r0 = pl.multiple_of(r * ROW_CHUNK, ROW_CHUNK)
            u_ref[pl.ds(halo + r0, ROW_CHUNK), :] = _rmsnorm(
                h_ref[pl.ds(r0, ROW_CHUNK), :], g2).astype(BF16)
            o_ref[pl.ds(r0, ROW_CHUNK), :] = jnp.zeros((ROW_CHUNK, o_ref.shape[1]), F32)
            return carry
        lax.fori_loop(0, tm // ROW_CHUNK, body, 0)

    def conv(s_ref, cw_ref, b_ref, r0, cs):
        return (s_ref[r0 + halo - 1:r0 + halo - 1 + FFN_ACT_ROWS, :] * cw_ref[0:1, cs]
                + s_ref[r0 + halo:r0 + halo + FFN_ACT_ROWS, :] * cw_ref[1:2, cs]
                + s_ref[r0 + halo + 1:r0 + halo + 1 + FFN_ACT_ROWS, :] * cw_ref[2:3, cs]) + b_ref[:, cs]

    def sub_cols(k):
        return slice(k * FFN_SUB, (k + 1) * FFN_SUB)

    def up(k, w_ref, s_refs):
        rows_all = halo + tm
        blk = -(-rows_all // (FFN_UP_SPLIT * BF16_ROWS)) * BF16_ROWS
        for r0 in range(0, rows_all, blk):
            rows = slice(r0, min(r0 + blk, rows_all))
            s_refs[k][rows, :] = jnp.dot(u_ref[rows, :], w_ref[:, sub_cols(k)],
                                         preferred_element_type=F32)
        s_refs[k][rows_all:rows_all + F32_ROWS, :] = s_refs[k][0:F32_ROWS, :]

    def activate(k):
        for r0 in range(0, tm, FFN_ACT_ROWS):
            a = conv(sa_refs[k], cwa_ref, ba_ref, r0, sub_cols(k))
            g = conv(sg_refs[k], cwg_ref, bgt_ref, r0, sub_cols(k))
            act_refs[k][r0:r0 + FFN_ACT_ROWS, :] = (a * jax.nn.sigmoid(a) * g).astype(BF16)

    def down(k):
        for c in range(o_ref.shape[1] // COL_CHUNK):
            cols = slice(c * COL_CHUNK, (c + 1) * COL_CHUNK)
            o_ref[:, cols] += jnp.dot(act_refs[k][...], wd_ref[sub_cols(k), cols],
                                      preferred_element_type=F32)

    for k in range(n_sub):
        up(k, wa_ref, sa_refs)
        up(k, wg_ref, sg_refs)
    for k in range(n_sub):
        activate(k)
        down(k)

    @pl.when(j == nf - 1)
    def _():
        def body(r, carry):
            rows = pl.ds(pl.multiple_of(r * ROW_CHUNK, ROW_CHUNK), ROW_CHUNK)
            h = h_ref[rows, :] + o_ref[rows, :]
            if final_norm:
                h = _rmsnorm(h, gf_ref[...])
            o_ref[rows, :] = h
            return carry
        lax.fori_loop(0, tm // ROW_CHUNK, body, 0)


def _ffn(h, g2, w_up, conv_w, conv_b, w_down, gf, final_norm):
    s, d = h.shape
    tm, tf = FFN_TM, FFN_TF
    nt, nf = s // tm, D_FF // tf
    hb = tm // BF16_ROWS
    n_sub = tf // FFN_SUB

    in_specs = [
        pl.BlockSpec((tm, d), lambda i, j: (i, 0)),
        pl.BlockSpec((BF16_ROWS, d), lambda i, j: (jnp.maximum(i * hb - 1, 0), 0)),
        pl.BlockSpec((BF16_ROWS, d), lambda i, j: (jnp.minimum((i + 1) * hb, s // BF16_ROWS - 1), 0)),
        pl.BlockSpec((1, d), lambda i, j: (0, 0)),
        pl.BlockSpec((d, tf), lambda i, j: (0, j)),
        pl.BlockSpec((d, tf), lambda i, j: (0, nf + j)),
        pl.BlockSpec((3, tf), lambda i, j: (0, j)),
        pl.BlockSpec((3, tf), lambda i, j: (0, nf + j)),
        pl.BlockSpec((1, tf), lambda i, j: (0, j)),
        pl.BlockSpec((1, tf), lambda i, j: (0, nf + j)),
        pl.BlockSpec((tf, d), lambda i, j: (j, 0)),
        pl.BlockSpec((1, d), lambda i, j: (0, 0)),
    ]
    return pl.pallas_call(
        functools.partial(_ffn_kernel, final_norm=final_norm),
        grid=(nt, nf),
        in_specs=in_specs,
        out_specs=pl.BlockSpec((tm, d), lambda i, j: (i, 0)),
        out_shape=jax.ShapeDtypeStruct((s, d), F32),
        scratch_shapes=[
            pltpu.VMEM((BF16_ROWS + tm, d), BF16),
        ] + [
            pltpu.VMEM((tm + 2 * BF16_ROWS, FFN_SUB), F32),
            pltpu.VMEM((tm + 2 * BF16_ROWS, FFN_SUB), F32),
            pltpu.VMEM((tm, FFN_SUB), BF16),
        ] * n_sub,
        compiler_params=pltpu.CompilerParams(
            dimension_semantics=("arbitrary", "arbitrary"),
            vmem_limit_bytes=VMEM_LIMIT_BYTES),
        name="ffn",
    )(h, h, h, g2, w_up, w_up, conv_w, conv_w, conv_b, conv_b, w_down, gf)


def _rope_tables(seq_len):
    half = HEAD_DIM // 2
    reps = LANES // half
    inv_freq = jnp.tile(ROPE_THETA ** (-jnp.arange(0, half, dtype=F32) / half), reps)[None, :]
    sign = jnp.tile(jnp.concatenate([-jnp.ones((half,), F32), jnp.ones((half,), F32)]),
                    reps // 2)
    base = (ROPE_SPAN * jnp.arange(seq_len // ROPE_SPAN, dtype=F32))[:, None] * inv_freq
    offs = jnp.arange(ROPE_SPAN, dtype=F32)[:, None] * inv_freq
    cb, sb = jnp.cos(base)[:, None, :], jnp.sin(base)[:, None, :]
    co, so = jnp.cos(offs)[None, :, :], jnp.sin(offs)[None, :, :]
    cos_t = (cb * co - sb * so).reshape(seq_len, LANES)
    sin_t = ((sb * co + cb * so) * sign).reshape(seq_len, LANES)
    return cos_t, sin_t


def kernel(x, norm_mix_g, w_in, b_gate, conv_a_w, w_out_a, sink_logits, w_o_attn, w_mix_out,
           norm_ffn_g, ffn_w_up, ffn_conv_w, ffn_conv_b, ffn_w_down, norm_final_g):
    b, s, d = x.shape
    depth = w_in.shape[0]
    cos_t, sin_t = _rope_tables(s)
    outs = []
    for bi in range(b):
        h = x[bi]
        for l in range(depth):
            z, (w_oa, w_o, w_mix) = _in_proj(
                h, norm_mix_g[l][None, :], w_in[l].astype(BF16),
                (w_out_a[l], w_o_attn[l], w_mix_out[l]))
            h, (w_up, w_down) = _mixer(
                h, z, cos_t, sin_t, sink_logits[l], conv_a_w[l], b_gate[l][None, :],
                w_oa, w_o, w_mix, (ffn_w_up[l], ffn_w_down[l]))
            h = _ffn(h, norm_ffn_g[l][None, :], w_up, ffn_conv_w[l], ffn_conv_b[l][None, :],
                     w_down, norm_final_g[None, :], final_norm=(l == depth - 1))
        outs.append(h[None])
    return outs[0] if b == 1 else jnp.concatenate(outs, axis=0)
```

```python
import functools

import jax
import jax.numpy as jnp
from jax import lax
from jax.experimental import pallas as pl
from jax.experimental.pallas import tpu as pltpu

D_MODEL = 2048
D_CONV = D_MODEL // 2
N_HEADS = 16
N_KV_HEADS = 4
HEAD_DIM = 64
D_ATTN = N_HEADS * HEAD_DIM
D_KV = N_KV_HEADS * HEAD_DIM
WINDOW = 128
BLOCK = 128
ROPE_THETA = 10000.0
D_FF = 5632
EPS = 1e-6

OFF_B = 0
OFF_C = OFF_B + D_CONV
OFF_VA = OFF_C + D_CONV
OFF_Q = OFF_VA + D_CONV
OFF_K = OFF_Q + D_ATTN
OFF_V = OFF_K + D_KV
OFF_GA = OFF_V + D_KV
OFF_GB = OFF_GA + D_MODEL
D_IN_PROJ = OFF_GB + D_MODEL
LOG2E = 1.4426950408889634
ROPE_SPAN = 256

LANES = 128
BF16_ROWS = 16
F32_ROWS = 8
VMEM_LIMIT_BYTES = 60000 * 1024
IN_VMEM_LIMIT_BYTES = 57 * 1024 * 1024
MIX_VMEM_LIMIT_BYTES = 52 * 1024 * 1024

IN_TM = 512
IN_TN = 4352
IN_NORM_ROWS = 256
MIX_TQ = 256
MIX_OUT_COLS = 256
FFN_TM = 1024
FFN_TF = 512
FFN_SUB = 256
FFN_ACT_ROWS = 128
FFN_UP_SPLIT = 3
ROW_CHUNK = 128
COL_CHUNK = 512

BF16 = jnp.bfloat16
F32 = jnp.float32


def _rmsnorm(x, g):
    ms = jnp.mean(x * x, axis=-1, keepdims=True)
    return x * lax.rsqrt(ms + EPS) * g


def _in_proj_kernel(x_ref, g_ref, w_ref, *rest, n_cast):
    src_refs = rest[:n_cast]
    z_ref = rest[n_cast]
    dst_refs = rest[n_cast + 1:2 * n_cast + 1]
    u_ref = rest[2 * n_cast + 1]

    for src, dst in zip(src_refs, dst_refs):
        dst[...] = src[...].astype(dst.dtype)

    @pl.when(pl.program_id(1) == 0)
    def _():
        for r0 in range(0, x_ref.shape[0], IN_NORM_ROWS):
            for r in range(r0, r0 + IN_NORM_ROWS, ROW_CHUNK):
                rows = slice(r, r + ROW_CHUNK)
                u_ref[rows, :] = _rmsnorm(x_ref[rows, :], g_ref[...]).astype(BF16)
            blk = slice(r0, r0 + IN_NORM_ROWS)
            z_ref[blk, :] = jnp.dot(u_ref[blk, :], w_ref[...],
                                    preferred_element_type=F32).astype(z_ref.dtype)

    @pl.when(pl.program_id(1) > 0)
    def _():
        z_ref[...] = jnp.dot(u_ref[...], w_ref[...],
                             preferred_element_type=F32).astype(z_ref.dtype)


def _slab_spec(shape, n_steps, step_fn):
    rows, cols = shape
    per = rows // n_steps
    hold = 1
    while per % BF16_ROWS:
        per, hold = per * 2, hold * 2
    assert per * (n_steps // hold) == rows
    return pl.BlockSpec((per, cols), lambda *idx: (step_fn(*idx) // hold, 0))


def _in_proj(x, g, w, cast_weights):
    s, d = x.shape
    n = w.shape[1]
    grid = (s // IN_TM, n // IN_TN)
    slabs = [_slab_spec(cw.shape, grid[0] * grid[1], lambda i, j: i * grid[1] + j)
             for cw in cast_weights]
    outs = pl.pallas_call(
        functools.partial(_in_proj_kernel, n_cast=len(cast_weights)),
        grid=grid,
        in_specs=[
            pl.BlockSpec((IN_TM, d), lambda i, j: (i, 0)),
            pl.BlockSpec((1, d), lambda i, j: (0, 0)),
            pl.BlockSpec((d, IN_TN), lambda i, j: (0, j)),
        ] + slabs,
        out_specs=[pl.BlockSpec((IN_TM, IN_TN), lambda i, j: (i, j))] + slabs,
        out_shape=[jax.ShapeDtypeStruct((s, n), BF16)]
        + [jax.ShapeDtypeStruct(cw.shape, BF16) for cw in cast_weights],
        scratch_shapes=[pltpu.VMEM((IN_TM, d), BF16)],
        compiler_params=pltpu.CompilerParams(
            dimension_semantics=("arbitrary", "arbitrary"),
            vmem_limit_bytes=IN_VMEM_LIMIT_BYTES),
        name="in_proj",
    )(x, g, w, *cast_weights)
    return outs[0], outs[1:]


def _rope(x, cos, sin_signed, first_half):
    partner = jnp.where(first_half, pltpu.roll(x, LANES - HEAD_DIM // 2, 1),
                        pltpu.roll(x, HEAD_DIM // 2, 1))
    return x * cos + partner * sin_signed


def _mixer_kernel(sink_ref, z_ref, kv_ref, kvp_ref, kvn_ref, cp_ref, vp_ref, cn_ref, vn_ref,
                  cos_ref, sin_ref, cosp_ref, sinp_ref, cosn_ref, sinn_ref, cw_ref, x_ref,
                  *rest, seq_len, n_cast):
    n_col = x_ref.shape[1] // COL_CHUNK
    gla_refs, glb_refs = rest[:n_col], rest[n_col:2 * n_col]
    bg_ref, woa_ref, wo_ref, wmix_ref = rest[2 * n_col:2 * n_col + 4]
    rest = rest[2 * n_col + 4:]
    src_refs = rest[:n_cast]
    h1_ref = rest[n_cast]
    dst_refs = rest[n_cast + 1:2 * n_cast + 1]
    (att_s_ref, ya_s_ref, qst_ref, kz_ref, vt_ref, cvs_ref,
     ys_a_ref, ys_b_ref, mix_ref) = rest[2 * n_cast + 1:]
    for src, dst in zip(src_refs, dst_refs):
        dst[...] = src[...].astype(dst.dtype)

    step = pl.program_id(0)
    nt = pl.num_programs(0) - 1
    i = jnp.minimum(step, nt - 1)
    slot = step % 2
    pslot = 1 - slot
    tq = z_ref.shape[0]
    d = x_ref.shape[1]

    @pl.when(step == 0)
    def _():
        att_s_ref[1] = jnp.zeros(att_s_ref.shape[1:], BF16)
        ya_s_ref[1] = jnp.zeros(ya_s_ref.shape[1:], BF16)

    def merge_stage1(c):
        cols = slice(c * COL_CHUNK, (c + 1) * COL_CHUNK)
        gcols = slice(d + c * COL_CHUNK, d + (c + 1) * COL_CHUNK)
        ys_a_ref[c % 2] = jnp.dot(ya_s_ref[pslot], woa_ref[:, cols], preferred_element_type=F32)
        ys_b_ref[c % 2] = jnp.dot(att_s_ref[pslot], wo_ref[:, cols], preferred_element_type=F32)
        for r0 in range(0, tq, ROW_CHUNK):
            rows = slice(r0, r0 + ROW_CHUNK)
            g_a = jax.nn.sigmoid(gla_refs[c][rows, :].astype(F32) + bg_ref[:, cols])
            g_b = jax.nn.sigmoid(glb_refs[c][rows, :].astype(F32) + bg_ref[:, gcols])
            mix_ref[rows, cols] = (g_a * ys_a_ref[c % 2, rows, :]
                                   + g_b * ys_b_ref[c % 2, rows, :]).astype(BF16)

    def merge_stage2(c):
        cols = slice(c * MIX_OUT_COLS, (c + 1) * MIX_OUT_COLS)
        h1_ref[:, cols] = x_ref[:, cols] + jnp.dot(mix_ref[...], wmix_ref[:, cols],
                                                   preferred_element_type=F32)

    for c in range(n_col):
        merge_stage1(c)
    merge_pieces = [functools.partial(merge_stage2, c) for c in range(d // MIX_OUT_COLS)]

    nb = tq // BLOCK
    grp = N_HEADS // N_KV_HEADS
    lane = lax.broadcasted_iota(jnp.int32, (1, LANES), 1)
    first_half = (lane % HEAD_DIM) < (HEAD_DIM // 2)
    low_head = lane < HEAD_DIM
    scale = HEAD_DIM ** -0.5 * LOG2E

    for b in range(nb):
        rows = slice(b * BLOCK, (b + 1) * BLOCK)
        cos_b = cos_ref[rows, :]
        sin_b = sin_ref[rows, :]
        for g in range(D_ATTN // LANES):
            h = g // 2
            xq = z_ref[rows, OFF_Q + g * LANES:OFF_Q + (g + 1) * LANES].astype(F32)
            xr = _rope(xq, cos_b, sin_b, first_half) * scale
            rolled = pltpu.roll(xr, HEAD_DIM, 1)
            even, odd = (xr, rolled) if h % 2 == 0 else (rolled, xr)
            j0 = 2 * (g % 2)
            qst_ref[h, b, j0 * BLOCK:(j0 + 1) * BLOCK, :] = even.astype(BF16)
            qst_ref[h, b, (j0 + 1) * BLOCK:(j0 + 2) * BLOCK, :] = odd.astype(BF16)

    sub = lax.broadcasted_iota(jnp.int32, (BF16_ROWS, BLOCK), 0)
    ones_rows = jnp.where(sub == 0, 1.0, 0.0).astype(F32)

    def put_kv(kv_blk_ref, c_ref, s_ref, blk, row0):
        src = slice(blk * BLOCK, (blk + 1) * BLOCK)
        dst = slice(row0, row0 + BLOCK)
        c = c_ref[src, :]
        s = s_ref[src, :]
        for pr in range(N_KV_HEADS // 2):
            kf = _rope(kv_blk_ref[src, pr * LANES:(pr + 1) * LANES].astype(F32), c, s, first_half)
            zero = jnp.zeros_like(kf)
            kz_ref[2 * pr, dst, :] = jnp.where(low_head, kf, zero).astype(BF16)
            kz_ref[2 * pr + 1, dst, :] = jnp.where(low_head, zero, kf).astype(BF16)
            vb_t = kv_blk_ref[src, D_KV + pr * LANES:D_KV + (pr + 1) * LANES].astype(F32).T
            vt_ref[2 * pr, :, dst] = jnp.concatenate(
                [vb_t[:HEAD_DIM], ones_rows], axis=0).astype(BF16)
            vt_ref[2 * pr + 1, :, dst] = jnp.concatenate(
                [vb_t[HEAD_DIM:], ones_rows], axis=0).astype(BF16)

    put_kv(kvp_ref, cosp_ref, sinp_ref, 0, 0)
    for blk in range(nb):
        put_kv(kv_ref, cos_ref, sin_ref, blk, (blk + 1) * BLOCK)
    put_kv(kvn_ref, cosn_ref, sinn_ref, 0, BLOCK + tq)

    c_io = lax.broadcasted_iota(jnp.int32, (BLOCK, BLOCK), 0)
    r_io = lax.broadcasted_iota(jnp.int32, (BLOCK, BLOCK), 1)

    def scores(b, h):
        return lax.dot_general(kz_ref[h, b * BLOCK:(b + 3) * BLOCK, :], qst_ref[h, b],
                               (((1,), (1,)), ((), ())),
                               preferred_element_type=F32)

    def finish(b, h, s_t):
        base = i * tq + b * BLOCK
        mask_lo = (c_io >= r_io) & (c_io + (base - BLOCK) >= 0)
        mask_hi = (c_io <= r_io) & (c_io + (base + BLOCK) < seq_len)
        probs, sink_terms = [], []
        for j in range(grp):
            sink = sink_ref[grp * h + j] * LOG2E
            cols = slice(j * BLOCK, (j + 1) * BLOCK)
            lo = jnp.where(mask_lo, s_t[0:BLOCK, cols], -jnp.inf)
            mid = s_t[BLOCK:2 * BLOCK, cols]
            hi = jnp.where(mask_hi, s_t[2 * BLOCK:3 * BLOCK, cols], -jnp.inf)
            m = jnp.maximum(jnp.maximum(jnp.max(lo, axis=0, keepdims=True),
                                        jnp.max(hi, axis=0, keepdims=True)),
                            jnp.maximum(jnp.max(mid, axis=0, keepdims=True), sink))
            probs.append(jnp.concatenate(
                [jnp.exp2(lo - m), jnp.exp2(mid - m), jnp.exp2(hi - m)], axis=0).astype(BF16))
            sink_terms.append(jnp.exp2(sink - m))
        p_t = jnp.concatenate(probs, axis=1)
        o_t = jnp.dot(vt_ref[h, :, b * BLOCK:(b + 3) * BLOCK], p_t,
                      preferred_element_type=F32)
        outs = []
        for j in range(grp):
            cols = slice(j * BLOCK, (j + 1) * BLOCK)
            denom = o_t[HEAD_DIM:HEAD_DIM + 1, cols] + sink_terms[j]
            outs.append(o_t[:HEAD_DIM, cols] * (1.0 / denom))
        for pr in range(grp // 2):
            pair_t = jnp.concatenate([outs[2 * pr], outs[2 * pr + 1]], axis=0)
            g = (grp // 2) * h + pr
            att_s_ref[slot, b * BLOCK:(b + 1) * BLOCK, g * LANES:(g + 1) * LANES] = (
                pair_t.T.astype(BF16))

    units = [(b, h) for b in range(nb) for h in range(N_KV_HEADS)]
    n_units, n_pieces = len(units), len(merge_pieces)
    s_next = scores(*units[0])
    for u, (b, h) in enumerate(units):
        s_t = s_next
        if u + 1 < n_units:
            s_next = scores(*units[u + 1])
        for piece in merge_pieces[u * n_pieces // n_units:(u + 1) * n_pieces // n_units]:
            piece()
        finish(b, h, s_t)

    zero8 = jnp.zeros((F32_ROWS, LANES), F32)
    for g in range(D_CONV // LANES):
        cols = slice(g * LANES, (g + 1) * LANES)
        zc = slice(OFF_C + g * LANES, OFF_C + (g + 1) * LANES)
        zv = slice(OFF_VA + g * LANES, OFF_VA + (g + 1) * LANES)
        zb = slice(OFF_B + g * LANES, OFF_B + (g + 1) * LANES)
        prev = (cp_ref[:, cols].astype(F32) * vp_ref[:, cols].astype(F32))[F32_ROWS:, :]
        nxt = (cn_ref[:, cols].astype(F32) * vn_ref[:, cols].astype(F32))[:F32_ROWS, :]
        cvs_ref[0:F32_ROWS, cols] = jnp.where(i > 0, prev, zero8)
        for r0 in range(0, tq, ROW_CHUNK):
            rows = slice(r0, r0 + ROW_CHUNK)
            cvs_ref[F32_ROWS + r0:F32_ROWS + r0 + ROW_CHUNK, cols] = (
                z_ref[rows, zc].astype(F32) * z_ref[rows, zv].astype(F32))
        cvs_ref[F32_ROWS + tq:2 * F32_ROWS + tq, cols] = jnp.where(i < nt - 1, nxt, zero8)
        for r0 in range(0, tq, ROW_CHUNK):
            lo = F32_ROWS + r0
            conv = (cvs_ref[lo - 1:lo - 1 + ROW_CHUNK, cols] * cw_ref[0:1, cols]
                    + cvs_ref[lo:lo + ROW_CHUNK, cols] * cw_ref[1:2, cols]
                    + cvs_ref[lo + 1:lo + 1 + ROW_CHUNK, cols] * cw_ref[2:3, cols])
            ya_s_ref[slot, r0:r0 + ROW_CHUNK, cols] = (
                z_ref[r0:r0 + ROW_CHUNK, zb].astype(F32) * conv).astype(BF16)


def _mixer(x, z, cos_t, sin_t, sink, conv_w, b_gate, w_out_a, w_o, w_mix, cast_weights):
    s, d = x.shape
    tq = MIX_TQ
    nt = s // tq
    kvb = tq // BLOCK
    cb = tq // BF16_ROWS
    qbcv = 4 * D_CONV
    kv_w = 2 * D_KV
    n_col = d // COL_CHUNK
    assert OFF_B == 0 and OFF_Q + D_ATTN == qbcv and OFF_K % kv_w == 0 and OFF_V == OFF_K + D_KV
    assert OFF_GA % COL_CHUNK == 0 and OFF_GB % COL_CHUNK == 0

    def br(step):
        return jnp.minimum(step, nt - 1)

    def mg(step):
        return jnp.maximum(step - 1, 0)

    def prev_blk(step, per):
        return jnp.maximum(br(step) * per - 1, 0)

    def next_blk(step, per, total):
        return jnp.minimum((br(step) + 1) * per, total - 1)

    def gate_spec(off, c):
        return pl.BlockSpec((tq, COL_CHUNK), lambda i: (mg(i), off // COL_CHUNK + c))

    const = lambda i: (0, 0)
    single = dict(pipeline_mode=pl.Buffered(1))
    in_specs = [
        pl.BlockSpec(memory_space=pltpu.SMEM),
        pl.BlockSpec((tq, qbcv), lambda i: (br(i), 0)),
        pl.BlockSpec((tq, kv_w), lambda i: (br(i), OFF_K // kv_w)),
        pl.BlockSpec((BLOCK, kv_w), lambda i: (prev_blk(i, kvb), OFF_K // kv_w)),
        pl.BlockSpec((BLOCK, kv_w), lambda i: (next_blk(i, kvb, s // BLOCK), OFF_K // kv_w)),
        pl.BlockSpec((BF16_ROWS, D_CONV), lambda i: (prev_blk(i, cb), OFF_C // D_CONV)),
        pl.BlockSpec((BF16_ROWS, D_CONV), lambda i: (prev_blk(i, cb), OFF_VA // D_CONV)),
        pl.BlockSpec((BF16_ROWS, D_CONV), lambda i: (next_blk(i, cb, s // BF16_ROWS), OFF_C // D_CONV)),
        pl.BlockSpec((BF16_ROWS, D_CONV), lambda i: (next_blk(i, cb, s // BF16_ROWS), OFF_VA // D_CONV)),
        pl.BlockSpec((tq, LANES), lambda i: (br(i), 0)),
        pl.BlockSpec((tq, LANES), lambda i: (br(i), 0)),
        pl.BlockSpec((BLOCK, LANES), lambda i: (prev_blk(i, kvb), 0)),
        pl.BlockSpec((BLOCK, LANES), lambda i: (prev_blk(i, kvb), 0)),
        pl.BlockSpec((BLOCK, LANES), lambda i: (next_blk(i, kvb, s // BLOCK), 0)),
        pl.BlockSpec((BLOCK, LANES), lambda i: (next_blk(i, kvb, s // BLOCK), 0)),
        pl.BlockSpec((3, D_CONV), const),
        pl.BlockSpec((tq, d), lambda i: (mg(i), 0)),
    ] + [gate_spec(OFF_GA, c) for c in range(n_col)] + [
        gate_spec(OFF_GB, c) for c in range(n_col)] + [
        pl.BlockSpec((1, 2 * d), const),
        pl.BlockSpec((D_CONV, d), const, **single),
        pl.BlockSpec((D_ATTN, d), const, **single),
        pl.BlockSpec((d, d), const, **single),
    ]
    r_ext = tq + 2 * BLOCK
    grp = N_HEADS // N_KV_HEADS
    slabs = [_slab_spec(cw.shape, nt, br) for cw in cast_weights]
    outs = pl.pallas_call(
        functools.partial(_mixer_kernel, seq_len=s, n_cast=len(cast_weights)),
        grid=(nt + 1,),
        in_specs=in_specs + slabs,
        out_specs=[pl.BlockSpec((tq, d), lambda i: (mg(i), 0))] + slabs,
        out_shape=[jax.ShapeDtypeStruct((s, d), F32)]
        + [jax.ShapeDtypeStruct(cw.shape, BF16) for cw in cast_weights],
        scratch_shapes=[
            pltpu.VMEM((2, tq, D_ATTN), BF16),
            pltpu.VMEM((2, tq, D_CONV), BF16),
            pltpu.VMEM((N_KV_HEADS, tq // BLOCK, grp * BLOCK, LANES), BF16),
            pltpu.VMEM((N_KV_HEADS, r_ext, LANES), BF16),
            pltpu.VMEM((N_KV_HEADS, HEAD_DIM + BF16_ROWS, r_ext), BF16),
            pltpu.VMEM((tq + 2 * F32_ROWS, D_CONV), F32),
            pltpu.VMEM((2, tq, COL_CHUNK), F32),
            pltpu.VMEM((2, tq, COL_CHUNK), F32),
            pltpu.VMEM((tq, d), BF16),
        ],
        compiler_params=pltpu.CompilerParams(
            dimension_semantics=("arbitrary",),
            vmem_limit_bytes=MIX_VMEM_LIMIT_BYTES),
        name="mixer",
    )(sink, z, z, z, z, z, z, z, z, cos_t, sin_t, cos_t, sin_t, cos_t, sin_t, conv_w, x,
      *([z] * (2 * n_col)), b_gate, w_out_a, w_o, w_mix, *cast_weights)
    return outs[0], outs[1:]


def _ffn_kernel(h_ref, hp_ref, hn_ref, g2_ref, wa_ref, wg_ref, cwa_ref, cwg_ref,
                ba_ref, bgt_ref, wd_ref, gf_ref,
                o_ref,
                u_ref, *sub_refs, final_norm):
    i = pl.program_id(0)
    j = pl.program_id(1)
    nt = pl.num_programs(0)
    nf = pl.num_programs(1)
    tm = h_ref.shape[0]
    halo = BF16_ROWS
    n_sub = len(sub_refs) // 3
    sa_refs, sg_refs, act_refs = sub_refs[0::3], sub_refs[1::3], sub_refs[2::3]

    @pl.when(j == 0)
    def _():
        g2 = g2_ref[...]
        zero = jnp.zeros((halo, h_ref.shape[1]), F32)
        prev_rows = jnp.where(i > 0, _rmsnorm(hp_ref[...], g2), zero)
        next_rows = jnp.where(i < nt - 1, _rmsnorm(hn_ref[...], g2), zero)
        first_row = lax.broadcasted_iota(jnp.int32, (halo, 1), 0) == 0
        u_ref[0:halo, :] = jnp.where(first_row, next_rows, prev_rows).astype(BF16)

        def body(r, carry):
            r0 = pl.multiple_of(r * ROW_CHUNK, ROW_CHUNK)
            u_ref[pl.ds(halo + r0, ROW_CHUNK), :] = _rmsnorm(
                h_ref[pl.ds(r0, ROW_CHUNK), :], g2).astype(BF16)
            o_ref[pl.ds(r0, ROW_CHUNK), :] = jnp.zeros((ROW_CHUNK, o_ref.shape[1]), F32)
            return carry
        lax.fori_loop(0, tm // ROW_CHUNK, body, 0)

    def conv(s_ref, cw_ref, b_ref, r0, cs):
        return (s_ref[r0 + halo - 1:r0 + halo - 1 + FFN_ACT_ROWS, :] * cw_ref[0:1, cs]
                + s_ref[r0 + halo:r0 + halo + FFN_ACT_ROWS, :] * cw_ref[1:2, cs]
                + s_ref[r0 + halo + 1:r0 + halo + 1 + FFN_ACT_ROWS, :] * cw_ref[2:3, cs]) + b_ref[:, cs]

    def sub_cols(k):
        return slice(k * FFN_SUB, (k + 1) * FFN_SUB)

    def up(k, w_ref, s_refs):
        rows_all = halo + tm
        blk = -(-rows_all // (FFN_UP_SPLIT * BF16_ROWS)) * BF16_ROWS
        for r0 in range(0, rows_all, blk):
            rows = slice(r0, min(r0 + blk, rows_all))
            s_refs[k][rows, :] = jnp.dot(u_ref[rows, :], w_ref[:, sub_cols(k)],
                                         preferred_element_type=F32)
        s_refs[k][rows_all:rows_all + F32_ROWS, :] = s_refs[k][0:F32_ROWS, :]

    def activate(k):
        for r0 in range(0, tm, FFN_ACT_ROWS):
            a = conv(sa_refs[k], cwa_ref, ba_ref, r0, sub_cols(k))
            g = conv(sg_refs[k], cwg_ref, bgt_ref, r0, sub_cols(k))
            act_refs[k][r0:r0 + FFN_ACT_ROWS, :] = (a * jax.nn.sigmoid(a) * g).astype(BF16)

    def down(k):
        for c in range(o_ref.shape[1] // COL_CHUNK):
            cols = slice(c * COL_CHUNK, (c + 1) * COL_CHUNK)
            o_ref[:, cols] += jnp.dot(act_refs[k][...], wd_ref[sub_cols(k), cols],
                                      preferred_element_type=F32)

    for k in range(n_sub):
        up(k, wa_ref, sa_refs)
        up(k, wg_ref, sg_refs)
    for k in range(n_sub):
        activate(k)
        down(k)

    @pl.when(j == nf - 1)
    def _():
        def body(r, carry):
            rows = pl.ds(pl.multiple_of(r * ROW_CHUNK, ROW_CHUNK), ROW_CHUNK)
            h = h_ref[rows, :] + o_ref[rows, :]
            if final_norm:
                h = _rmsnorm(h, gf_ref[...])
            o_ref[rows, :] = h
            return carry
        lax.fori_loop(0, tm // ROW_CHUNK, body, 0)


def _ffn(h, g2, w_up, conv_w, conv_b, w_down, gf, final_norm):
    s, d = h.shape
    tm, tf = FFN_TM, FFN_TF
    nt, nf = s // tm, D_FF // tf
    hb = tm // BF16_ROWS
    n_sub = tf // FFN_SUB

    in_specs = [
        pl.BlockSpec((tm, d), lambda i, j: (i, 0)),
        pl.BlockSpec((BF16_ROWS, d), lambda i, j: (jnp.maximum(i * hb - 1, 0), 0)),
        pl.BlockSpec((BF16_ROWS, d), lambda i, j: (jnp.minimum((i + 1) * hb, s // BF16_ROWS - 1), 0)),
        pl.BlockSpec((1, d), lambda i, j: (0, 0)),
        pl.BlockSpec((d, tf), lambda i, j: (0, j)),
        pl.BlockSpec((d, tf), lambda i, j: (0, nf + j)),
        pl.BlockSpec((3, tf), lambda i, j: (0, j)),
        pl.BlockSpec((3, tf), lambda i, j: (0, nf + j)),
        pl.BlockSpec((1, tf), lambda i, j: (0, j)),
        pl.BlockSpec((1, tf), lambda i, j: (0, nf + j)),
        pl.BlockSpec((tf, d), lambda i, j: (j, 0)),
        pl.BlockSpec((1, d), lambda i, j: (0, 0)),
    ]
    return pl.pallas_call(
        functools.partial(_ffn_kernel, final_norm=final_norm),
        grid=(nt, nf),
        in_specs=in_specs,
        out_specs=pl.BlockSpec((tm, d), lambda i, j: (i, 0)),
        out_shape=jax.ShapeDtypeStruct((s, d), F32),
        scratch_shapes=[
            pltpu.VMEM((BF16_ROWS + tm, d), BF16),
        ] + [
            pltpu.VMEM((tm + 2 * BF16_ROWS, FFN_SUB), F32),
            pltpu.VMEM((tm + 2 * BF16_ROWS, FFN_SUB), F32),
            pltpu.VMEM((tm, FFN_SUB), BF16),
        ] * n_sub,
        compiler_params=pltpu.CompilerParams(
            dimension_semantics=("arbitrary", "arbitrary"),
            vmem_limit_bytes=VMEM_LIMIT_BYTES),
        name="ffn",
    )(h, h, h, g2, w_up, w_up, conv_w, conv_w, conv_b, conv_b, w_down, gf)


def _rope_tables(seq_len):
    half = HEAD_DIM // 2
    reps = LANES // half
    inv_freq = jnp.tile(ROPE_THETA ** (-jnp.arange(0, half, dtype=F32) / half), reps)[None, :]
    sign = jnp.tile(jnp.concatenate([-jnp.ones((half,), F32), jnp.ones((half,), F32)]),
                    reps // 2)
    base = (ROPE_SPAN * jnp.arange(seq_len // ROPE_SPAN, dtype=F32))[:, None] * inv_freq
    offs = jnp.arange(ROPE_SPAN, dtype=F32)[:, None] * inv_freq
    cb, sb = jnp.cos(base)[:, None, :], jnp.sin(base)[:, None, :]
    co, so = jnp.cos(offs)[None, :, :], jnp.sin(offs)[None, :, :]
    cos_t = (cb * co - sb * so).reshape(seq_len, LANES)
    sin_t = ((sb * co + cb * so) * sign).reshape(seq_len, LANES)
    return cos_t, sin_t


def kernel(x, norm_mix_g, w_in, b_gate, conv_a_w, w_out_a, sink_logits, w_o_attn, w_mix_out,
           norm_ffn_g, ffn_w_up, ffn_conv_w, ffn_conv_b, ffn_w_down, norm_final_g):
    b, s, d = x.shape
    depth = w_in.shape[0]
    cos_t, sin_t = _rope_tables(s)
    outs = []
    for bi in range(b):
        h = x[bi]
        for l in range(depth):
            z, (w_oa, w_o, w_mix) = _in_proj(
                h, norm_mix_g[l][None, :], w_in[l].astype(BF16),
                (w_out_a[l], w_o_attn[l], w_mix_out[l]))
            h, (w_up, w_down) = _mixer(
                h, z, cos_t, sin_t, sink_logits[l], conv_a_w[l], b_gate[l][None, :],
                w_oa, w_o, w_mix, (ffn_w_up[l], ffn_w_down[l]))
            h = _ffn(h, norm_ffn_g[l][None, :], w_up, ffn_conv_w[l], ffn_conv_b[l][None, :],
                     w_down, norm_final_g[None, :], final_norm=(l == depth - 1))
        outs.append(h[None])
    return outs[0] if b == 1 else jnp.concatenate(outs, axis=0)
```

```python
import functools

import jax
import jax.numpy as jnp
from jax import lax
from jax.experimental import pallas as pl
from jax.experimental.pallas import tpu as pltpu

D_MODEL = 2048
D_CONV = D_MODEL // 2
N_HEADS = 16
N_KV_HEADS = 4
HEAD_DIM = 64
D_ATTN = N_HEADS * HEAD_DIM
D_KV = N_KV_HEADS * HEAD_DIM
WINDOW = 128
BLOCK = 128
ROPE_THETA = 10000.0
D_FF = 5632
EPS = 1e-6

OFF_B = 0
OFF_C = OFF_B + D_CONV
OFF_VA = OFF_C + D_CONV
OFF_Q = OFF_VA + D_CONV
OFF_K = OFF_Q + D_ATTN
OFF_V = OFF_K + D_KV
OFF_GA = OFF_V + D_KV
OFF_GB = OFF_GA + D_MODEL
D_IN_PROJ = OFF_GB + D_MODEL
LOG2E = 1.4426950408889634
ROPE_SPAN = 256

LANES = 128
BF16_ROWS = 16
F32_ROWS = 8
VMEM_LIMIT_BYTES = 60000 * 1024
IN_VMEM_LIMIT_BYTES = 57 * 1024 * 1024
MIX_VMEM_LIMIT_BYTES = 52 * 1024 * 1024

IN_TM = 512
IN_TN = 4352
IN_NORM_ROWS = 256
MIX_TQ = 256
MIX_OUT_COLS = 256
MIX_DOT_ROWS = 128
FFN_TM = 1024
FFN_TF = 512
FFN_SUB = 256
FFN_ACT_ROWS = 128
FFN_UP_SPLIT = 3
ROW_CHUNK = 128
COL_CHUNK = 512

BF16 = jnp.bfloat16
F32 = jnp.float32


def _rmsnorm(x, g):
    ms = jnp.mean(x * x, axis=-1, keepdims=True)
    return x * lax.rsqrt(ms + EPS) * g


def _in_proj_kernel(x_ref, g_ref, w_ref, *rest, n_cast):
    src_refs = rest[:n_cast]
    z_ref = rest[n_cast]
    dst_refs = rest[n_cast + 1:2 * n_cast + 1]
    u_ref = rest[2 * n_cast + 1]

    for src, dst in zip(src_refs, dst_refs):
        dst[...] = src[...].astype(dst.dtype)

    @pl.when(pl.program_id(1) == 0)
    def _():
        for r0 in range(0, x_ref.shape[0], IN_NORM_ROWS):
            for r in range(r0, r0 + IN_NORM_ROWS, ROW_CHUNK):
                rows = slice(r, r + ROW_CHUNK)
                u_ref[rows, :] = _rmsnorm(x_ref[rows, :], g_ref[...]).astype(BF16)
            blk = slice(r0, r0 + IN_NORM_ROWS)
            z_ref[blk, :] = jnp.dot(u_ref[blk, :], w_ref[...],
                                    preferred_element_type=F32).astype(z_ref.dtype)

    @pl.when(pl.program_id(1) > 0)
    def _():
        z_ref[...] = jnp.dot(u_ref[...], w_ref[...],
                             preferred_element_type=F32).astype(z_ref.dtype)


def _slab_spec(shape, n_steps, step_fn):
    rows, cols = shape
    per = rows // n_steps
    hold = 1
    while per % BF16_ROWS:
        per, hold = per * 2, hold * 2
    assert per * (n_steps // hold) == rows
    return pl.BlockSpec((per, cols), lambda *idx: (step_fn(*idx) // hold, 0))


def _in_proj(x, g, w, cast_weights):
    s, d = x.shape
    n = w.shape[1]
    grid = (s // IN_TM, n // IN_TN)
    slabs = [_slab_spec(cw.shape, grid[0] * grid[1], lambda i, j: i * grid[1] + j)
             for cw in cast_weights]
    outs = pl.pallas_call(
        functools.partial(_in_proj_kernel, n_cast=len(cast_weights)),
        grid=grid,
        in_specs=[
            pl.BlockSpec((IN_TM, d), lambda i, j: (i, 0)),
            pl.BlockSpec((1, d), lambda i, j: (0, 0)),
            pl.BlockSpec((d, IN_TN), lambda i, j: (0, j)),
        ] + slabs,
        out_specs=[pl.BlockSpec((IN_TM, IN_TN), lambda i, j: (i, j))] + slabs,
        out_shape=[jax.ShapeDtypeStruct((s, n), BF16)]
        + [jax.ShapeDtypeStruct(cw.shape, BF16) for cw in cast_weights],
        scratch_shapes=[pltpu.VMEM((IN_TM, d), BF16)],
        compiler_params=pltpu.CompilerParams(
            dimension_semantics=("arbitrary", "arbitrary"),
            vmem_limit_bytes=IN_VMEM_LIMIT_BYTES),
        name="in_proj",
    )(x, g, w, *cast_weights)
    return outs[0], outs[1:]


def _rope(x, cos, sin_signed, first_half):
    partner = jnp.where(first_half, pltpu.roll(x, LANES - HEAD_DIM // 2, 1),
                        pltpu.roll(x, HEAD_DIM // 2, 1))
    return x * cos + partner * sin_signed


def _mixer_kernel(sink_ref, z_ref, kv_ref, kvp_ref, kvn_ref, cp_ref, vp_ref, cn_ref, vn_ref,
                  cos_ref, sin_ref, cosp_ref, sinp_ref, cosn_ref, sinn_ref, cw_ref, x_ref,
                  *rest, seq_len, n_cast):
    n_col = x_ref.shape[1] // COL_CHUNK
    gla_refs, glb_refs = rest[:n_col], rest[n_col:2 * n_col]
    bg_ref, woa_ref, wo_ref, wmix_ref = rest[2 * n_col:2 * n_col + 4]
    rest = rest[2 * n_col + 4:]
    src_refs = rest[:n_cast]
    h1_ref = rest[n_cast]
    dst_refs = rest[n_cast + 1:2 * n_cast + 1]
    (att_s_ref, ya_s_ref, qst_ref, kz_ref, vt_ref, cvs_ref,
     ys_a_ref, ys_b_ref, mix_ref) = rest[2 * n_cast + 1:]
    for src, dst in zip(src_refs, dst_refs):
        dst[...] = src[...].astype(dst.dtype)

    step = pl.program_id(0)
    nt = pl.num_programs(0) - 1
    i = jnp.minimum(step, nt - 1)
    slot = step % 2
    pslot = 1 - slot
    tq = z_ref.shape[0]
    d = x_ref.shape[1]

    @pl.when(step == 0)
    def _():
        att_s_ref[1] = jnp.zeros(att_s_ref.shape[1:], BF16)
        ya_s_ref[1] = jnp.zeros(ya_s_ref.shape[1:], BF16)

    def merge_stage1(c):
        cols = slice(c * COL_CHUNK, (c + 1) * COL_CHUNK)
        gcols = slice(d + c * COL_CHUNK, d + (c + 1) * COL_CHUNK)
        ys_a_ref[c % 2] = jnp.dot(ya_s_ref[pslot], woa_ref[:, cols], preferred_element_type=F32)
        ys_b_ref[c % 2] = jnp.dot(att_s_ref[pslot], wo_ref[:, cols], preferred_element_type=F32)
        for r0 in range(0, tq, ROW_CHUNK):
            rows = slice(r0, r0 + ROW_CHUNK)
            g_a = jax.nn.sigmoid(gla_refs[c][rows, :].astype(F32) + bg_ref[:, cols])
            g_b = jax.nn.sigmoid(glb_refs[c][rows, :].astype(F32) + bg_ref[:, gcols])
            mix_ref[rows, cols] = (g_a * ys_a_ref[c % 2, rows, :]
                                   + g_b * ys_b_ref[c % 2, rows, :]).astype(BF16)

    def merge_stage2(c):
        cols = slice(c * MIX_OUT_COLS, (c + 1) * MIX_OUT_COLS)
        for r0 in range(0, tq, MIX_DOT_ROWS):
            blk = slice(r0, r0 + MIX_DOT_ROWS)
            h1_ref[blk, cols] = x_ref[blk, cols] + jnp.dot(mix_ref[blk, :], wmix_ref[:, cols],
                                                           preferred_element_type=F32)

    for c in range(n_col):
        merge_stage1(c)
    merge_pieces = [functools.partial(merge_stage2, c) for c in range(d // MIX_OUT_COLS)]

    nb = tq // BLOCK
    grp = N_HEADS // N_KV_HEADS
    lane = lax.broadcasted_iota(jnp.int32, (1, LANES), 1)
    first_half = (lane % HEAD_DIM) < (HEAD_DIM // 2)
    low_head = lane < HEAD_DIM
    scale = HEAD_DIM ** -0.5 * LOG2E

    for b in range(nb):
        rows = slice(b * BLOCK, (b + 1) * BLOCK)
        cos_b = cos_ref[rows, :]
        sin_b = sin_ref[rows, :]
        for g in range(D_ATTN // LANES):
            h = g // 2
            xq = z_ref[rows, OFF_Q + g * LANES:OFF_Q + (g + 1) * LANES].astype(F32)
            xr = _rope(xq, cos_b, sin_b, first_half) * scale
            rolled = pltpu.roll(xr, HEAD_DIM, 1)
            even, odd = (xr, rolled) if h % 2 == 0 else (rolled, xr)
            j0 = 2 * (g % 2)
            qst_ref[h, b, j0 * BLOCK:(j0 + 1) * BLOCK, :] = even.astype(BF16)
            qst_ref[h, b, (j0 + 1) * BLOCK:(j0 + 2) * BLOCK, :] = odd.astype(BF16)

    sub = lax.broadcasted_iota(jnp.int32, (BF16_ROWS, BLOCK), 0)
    ones_rows = jnp.where(sub == 0, 1.0, 0.0).astype(F32)

    def put_kv(kv_blk_ref, c_ref, s_ref, blk, row0):
        src = slice(blk * BLOCK, (blk + 1) * BLOCK)
        dst = slice(row0, row0 + BLOCK)
        c = c_ref[src, :]
        s = s_ref[src, :]
        for pr in range(N_KV_HEADS // 2):
            kf = _rope(kv_blk_ref[src, pr * LANES:(pr + 1) * LANES].astype(F32), c, s, first_half)
            zero = jnp.zeros_like(kf)
            kz_ref[2 * pr, dst, :] = jnp.where(low_head, kf, zero).astype(BF16)
            kz_ref[2 * pr + 1, dst, :] = jnp.where(low_head, zero, kf).astype(BF16)
            vb_t = kv_blk_ref[src, D_KV + pr * LANES:D_KV + (pr + 1) * LANES].astype(F32).T
            vt_ref[2 * pr, :, dst] = jnp.concatenate(
                [vb_t[:HEAD_DIM], ones_rows], axis=0).astype(BF16)
            vt_ref[2 * pr + 1, :, dst] = jnp.concatenate(
                [vb_t[HEAD_DIM:], ones_rows], axis=0).astype(BF16)

    put_kv(kvp_ref, cosp_ref, sinp_ref, 0, 0)
    for blk in range(nb):
        put_kv(kv_ref, cos_ref, sin_ref, blk, (blk + 1) * BLOCK)
    put_kv(kvn_ref, cosn_ref, sinn_ref, 0, BLOCK + tq)

    c_io = lax.broadcasted_iota(jnp.int32, (BLOCK, BLOCK), 0)
    r_io = lax.broadcasted_iota(jnp.int32, (BLOCK, BLOCK), 1)

    def scores(b, h):
        return lax.dot_general(kz_ref[h, b * BLOCK:(b + 3) * BLOCK, :], qst_ref[h, b],
                               (((1,), (1,)), ((), ())),
                               preferred_element_type=F32)

    def finish(b, h, s_t):
        base = i * tq + b * BLOCK
        mask_lo = (c_io >= r_io) & (c_io + (base - BLOCK) >= 0)
        mask_hi = (c_io <= r_io) & (c_io + (base + BLOCK) < seq_len)
        probs, sink_terms = [], []
        for j in range(grp):
            sink = sink_ref[grp * h + j] * LOG2E
            cols = slice(j * BLOCK, (j + 1) * BLOCK)
            lo = jnp.where(mask_lo, s_t[0:BLOCK, cols], -jnp.inf)
            mid = s_t[BLOCK:2 * BLOCK, cols]
            hi = jnp.where(mask_hi, s_t[2 * BLOCK:3 * BLOCK, cols], -jnp.inf)
            m = jnp.maximum(jnp.maximum(jnp.max(lo, axis=0, keepdims=True),
                                        jnp.max(hi, axis=0, keepdims=True)),
                            jnp.maximum(jnp.max(mid, axis=0, keepdims=True), sink))
            probs.append(jnp.concatenate(
                [jnp.exp2(lo - m), jnp.exp2(mid - m), jnp.exp2(hi - m)], axis=0).astype(BF16))
            sink_terms.append(jnp.exp2(sink - m))
        p_t = jnp.concatenate(probs, axis=1)
        o_t = jnp.dot(vt_ref[h, :, b * BLOCK:(b + 3) * BLOCK], p_t,
                      preferred_element_type=F32)
        outs = []
        for j in range(grp):
            cols = slice(j * BLOCK, (j + 1) * BLOCK)
            denom = o_t[HEAD_DIM:HEAD_DIM + 1, cols] + sink_terms[j]
            outs.append(o_t[:HEAD_DIM, cols] * (1.0 / denom))
        for pr in range(grp // 2):
            pair_t = jnp.concatenate([outs[2 * pr], outs[2 * pr + 1]], axis=0)
            g = (grp // 2) * h + pr
            att_s_ref[slot, b * BLOCK:(b + 1) * BLOCK, g * LANES:(g + 1) * LANES] = (
                pair_t.T.astype(BF16))

    units = [(b, h) for b in range(nb) for h in range(N_KV_HEADS)]
    n_units, n_pieces = len(units), len(merge_pieces)
    s_next = scores(*units[0])
    for u, (b, h) in enumerate(units):
        s_t = s_next
        if u + 1 < n_units:
            s_next = scores(*units[u + 1])
        for piece in merge_pieces[u * n_pieces // n_units:(u + 1) * n_pieces // n_units]:
            piece()
        finish(b, h, s_t)

    zero8 = jnp.zeros((F32_ROWS, LANES), F32)
    for g in range(D_CONV // LANES):
        cols = slice(g * LANES, (g + 1) * LANES)
        zc = slice(OFF_C + g * LANES, OFF_C + (g + 1) * LANES)
        zv = slice(OFF_VA + g * LANES, OFF_VA + (g + 1) * LANES)
        zb = slice(OFF_B + g * LANES, OFF_B + (g + 1) * LANES)
        prev = (cp_ref[:, cols].astype(F32) * vp_ref[:, cols].astype(F32))[F32_ROWS:, :]
        nxt = (cn_ref[:, cols].astype(F32) * vn_ref[:, cols].astype(F32))[:F32_ROWS, :]
        cvs_ref[0:F32_ROWS, cols] = jnp.where(i > 0, prev, zero8)
        for r0 in range(0, tq, ROW_CHUNK):
            rows = slice(r0, r0 + ROW_CHUNK)
            cvs_ref[F32_ROWS + r0:F32_ROWS + r0 + ROW_CHUNK, cols] = (
                z_ref[rows, zc].astype(F32) * z_ref[rows, zv].astype(F32))
        cvs_ref[F32_ROWS + tq:2 * F32_ROWS + tq, cols] = jnp.where(i < nt - 1, nxt, zero8)
        for r0 in range(0, tq, ROW_CHUNK):
            lo = F32_ROWS + r0
            conv = (cvs_ref[lo - 1:lo - 1 + ROW_CHUNK, cols] * cw_ref[0:1, cols]
                    + cvs_ref[lo:lo + ROW_CHUNK, cols] * cw_ref[1:2, cols]
                    + cvs_ref[lo + 1:lo + 1 + ROW_CHUNK, cols] * cw_ref[2:3, cols])
            ya_s_ref[slot, r0:r0 + ROW_CHUNK, cols] = (
                z_ref[r0:r0 + ROW_CHUNK, zb].astype(F32) * conv).astype(BF16)


def _mixer(x, z, cos_t, sin_t, sink, conv_w, b_gate, w_out_a, w_o, w_mix, cast_weights):
    s, d = x.shape
    tq = MIX_TQ
    nt = s // tq
    kvb = tq // BLOCK
    cb = tq // BF16_ROWS
    qbcv = 4 * D_CONV
    kv_w = 2 * D_KV
    n_col = d // COL_CHUNK
    assert OFF_B == 0 and OFF_Q + D_ATTN == qbcv and OFF_K % kv_w == 0 and OFF_V == OFF_K + D_KV
    assert OFF_GA % COL_CHUNK == 0 and OFF_GB % COL_CHUNK == 0

    def br(step):
        return jnp.minimum(step, nt - 1)

    def mg(step):
        return jnp.maximum(step - 1, 0)

    def prev_blk(step, per):
        return jnp.maximum(br(step) * per - 1, 0)

    def next_blk(step, per, total):
        return jnp.minimum((br(step) + 1) * per, total - 1)

    def gate_spec(off, c):
        return pl.BlockSpec((tq, COL_CHUNK), lambda i: (mg(i), off // COL_CHUNK + c))

    const = lambda i: (0, 0)
    single = dict(pipeline_mode=pl.Buffered(1))
    in_specs = [
        pl.BlockSpec(memory_space=pltpu.SMEM),
        pl.BlockSpec((tq, qbcv), lambda i: (br(i), 0)),
        pl.BlockSpec((tq, kv_w), lambda i: (br(i), OFF_K // kv_w)),
        pl.BlockSpec((BLOCK, kv_w), lambda i: (prev_blk(i, kvb), OFF_K // kv_w)),
        pl.BlockSpec((BLOCK, kv_w), lambda i: (next_blk(i, kvb, s // BLOCK), OFF_K // kv_w)),
        pl.BlockSpec((BF16_ROWS, D_CONV), lambda i: (prev_blk(i, cb), OFF_C // D_CONV)),
        pl.BlockSpec((BF16_ROWS, D_CONV), lambda i: (prev_blk(i, cb), OFF_VA // D_CONV)),
        pl.BlockSpec((BF16_ROWS, D_CONV), lambda i: (next_blk(i, cb, s // BF16_ROWS), OFF_C // D_CONV)),
        pl.BlockSpec((BF16_ROWS, D_CONV), lambda i: (next_blk(i, cb, s // BF16_ROWS), OFF_VA // D_CONV)),
        pl.BlockSpec((tq, LANES), lambda i: (br(i), 0)),
        pl.BlockSpec((tq, LANES), lambda i: (br(i), 0)),
        pl.BlockSpec((BLOCK, LANES), lambda i: (prev_blk(i, kvb), 0)),
        pl.BlockSpec((BLOCK, LANES), lambda i: (prev_blk(i, kvb), 0)),
        pl.BlockSpec((BLOCK, LANES), lambda i: (next_blk(i, kvb, s // BLOCK), 0)),
        pl.BlockSpec((BLOCK, LANES), lambda i: (next_blk(i, kvb, s // BLOCK), 0)),
        pl.BlockSpec((3, D_CONV), const),
        pl.BlockSpec((tq, d), lambda i: (mg(i), 0)),
    ] + [gate_spec(OFF_GA, c) for c in range(n_col)] + [
        gate_spec(OFF_GB, c) for c in range(n_col)] + [
        pl.BlockSpec((1, 2 * d), const),
        pl.BlockSpec((D_CONV, d), const, **single),
        pl.BlockSpec((D_ATTN, d), const, **single),
        pl.BlockSpec((d, d), const, **single),
    ]
    r_ext = tq + 2 * BLOCK
    grp = N_HEADS // N_KV_HEADS
    slabs = [_slab_spec(cw.shape, nt, br) for cw in cast_weights]
    outs = pl.pallas_call(
        functools.partial(_mixer_kernel, seq_len=s, n_cast=len(cast_weights)),
        grid=(nt + 1,),
        in_specs=in_specs + slabs,
        out_specs=[pl.BlockSpec((tq, d), lambda i: (mg(i), 0))] + slabs,
        out_shape=[jax.ShapeDtypeStruct((s, d), F32)]
        + [jax.ShapeDtypeStruct(cw.shape, BF16) for cw in cast_weights],
        scratch_shapes=[
            pltpu.VMEM((2, tq, D_ATTN), BF16),
            pltpu.VMEM((2, tq, D_CONV), BF16),
            pltpu.VMEM((N_KV_HEADS, tq // BLOCK, grp * BLOCK, LANES), BF16),
            pltpu.VMEM((N_KV_HEADS, r_ext, LANES), BF16),
            pltpu.VMEM((N_KV_HEADS, HEAD_DIM + BF16_ROWS, r_ext), BF16),
            pltpu.VMEM((tq + 2 * F32_ROWS, D_CONV), F32),
            pltpu.VMEM((2, tq, COL_CHUNK), F32),
            pltpu.VMEM((2, tq, COL_CHUNK), F32),
            pltpu.VMEM((tq, d), BF16),
        ],
        compiler_params=pltpu.CompilerParams(
            dimension_semantics=("arbitrary",),
            vmem_limit_bytes=MIX_VMEM_LIMIT_BYTES),
        name="mixer",
    )(sink, z, z, z, z, z, z, z, z, cos_t, sin_t, cos_t, sin_t, cos_t, sin_t, conv_w, x,
      *([z] * (2 * n_col)), b_gate, w_out_a, w_o, w_mix, *cast_weights)
    return outs[0], outs[1:]


def _ffn_kernel(h_ref, hp_ref, hn_ref, g2_ref, wa_ref, wg_ref, cwa_ref, cwg_ref,
                ba_ref, bgt_ref, wd_ref, gf_ref,
                o_ref,
                u_ref, *sub_refs, final_norm):
    i = pl.program_id(0)
    j = pl.program_id(1)
    nt = pl.num_programs(0)
    nf = pl.num_programs(1)
    tm = h_ref.shape[0]
    halo = BF16_ROWS
    n_sub = len(sub_refs) // 3
    sa_refs, sg_refs, act_refs = sub_refs[0::3], sub_refs[1::3], sub_refs[2::3]

    @pl.when(j == 0)
    def _():
        g2 = g2_ref[...]
        zero = jnp.zeros((halo, h_ref.shape[1]), F32)
        prev_rows = jnp.where(i > 0, _rmsnorm(hp_ref[...], g2), zero)
        next_rows = jnp.where(i < nt - 1, _rmsnorm(hn_ref[...], g2), zero)
        first_row = lax.broadcasted_iota(jnp.int32, (halo, 1), 0) == 0
        u_ref[0:halo, :] = jnp.where(first_row, next_rows, prev_rows).astype(BF16)

        def body(r, carry):
            r0 = pl.multiple_of(r * ROW_CHUNK, ROW_CHUNK)
            u_ref[pl.ds(halo + r0, ROW_CHUNK), :] = _rmsnorm(
                h_ref[pl.ds(r0, ROW_CHUNK), :], g2).astype(BF16)
            o_ref[pl.ds(r0, ROW_CHUNK), :] = jnp.zeros((ROW_CHUNK, o_ref.shape[1]), F32)
            return carry
        lax.fori_loop(0, tm // ROW_CHUNK, body, 0)

    def conv(s_ref, cw_ref, b_ref, r0, cs):
        return (s_ref[r0 + halo - 1:r0 + halo - 1 + FFN_ACT_ROWS, :] * cw_ref[0:1, cs]
                + s_ref[r0 + halo:r0 + halo + FFN_ACT_ROWS, :] * cw_ref[1:2, cs]
                + s_ref[r0 + halo + 1:r0 + halo + 1 + FFN_ACT_ROWS, :] * cw_ref[2:3, cs]) + b_ref[:, cs]

    def sub_cols(k):
        return slice(k * FFN_SUB, (k + 1) * FFN_SUB)

    def up(k, w_ref, s_refs):
        rows_all = halo + tm
        blk = -(-rows_all // (FFN_UP_SPLIT * BF16_ROWS)) * BF16_ROWS
        for r0 in range(0, rows_all, blk):
            rows = slice(r0, min(r0 + blk, rows_all))
            s_refs[k][rows, :] = jnp.dot(u_ref[rows, :], w_ref[:, sub_cols(k)],
                                         preferred_element_type=F32)
        s_refs[k][rows_all:rows_all + F32_ROWS, :] = s_refs[k][0:F32_ROWS, :]

    def activate(k):
        for r0 in range(0, tm, FFN_ACT_ROWS):
            a = conv(sa_refs[k], cwa_ref, ba_ref, r0, sub_cols(k))
            g = conv(sg_refs[k], cwg_ref, bgt_ref, r0, sub_cols(k))
            act_refs[k][r0:r0 + FFN_ACT_ROWS, :] = (a * jax.nn.sigmoid(a) * g).astype(BF16)

    def down(k):
        for c in range(o_ref.shape[1] // COL_CHUNK):
            cols = slice(c * COL_CHUNK, (c + 1) * COL_CHUNK)
            o_ref[:, cols] += jnp.dot(act_refs[k][...], wd_ref[sub_cols(k), cols],
                                      preferred_element_type=F32)

    for k in range(n_sub):
        up(k, wa_ref, sa_refs)
        up(k, wg_ref, sg_refs)
    for k in range(n_sub):
        activate(k)
        down(k)

    @pl.when(j == nf - 1)
    def _():
        def body(r, carry):
            rows = pl.ds(pl.multiple_of(r * ROW_CHUNK, ROW_CHUNK), ROW_CHUNK)
            h = h_ref[rows, :] + o_ref[rows, :]
            if final_norm:
                h = _rmsnorm(h, gf_ref[...])
            o_ref[rows, :] = h
            return carry
        lax.fori_loop(0, tm // ROW_CHUNK, body, 0)


def _ffn(h, g2, w_up, conv_w, conv_b, w_down, gf, final_norm):
    s, d = h.shape
    tm, tf = FFN_TM, FFN_TF
    nt, nf = s // tm, D_FF // tf
    hb = tm // BF16_ROWS
    n_sub = tf // FFN_SUB

    in_specs = [
        pl.BlockSpec((tm, d), lambda i, j: (i, 0)),
        pl.BlockSpec((BF16_ROWS, d), lambda i, j: (jnp.maximum(i * hb - 1, 0), 0)),
        pl.BlockSpec((BF16_ROWS, d), lambda i, j: (jnp.minimum((i + 1) * hb, s // BF16_ROWS - 1), 0)),
        pl.BlockSpec((1, d), lambda i, j: (0, 0)),
        pl.BlockSpec((d, tf), lambda i, j: (0, j)),
        pl.BlockSpec((d, tf), lambda i, j: (0, nf + j)),
        pl.BlockSpec((3, tf), lambda i, j: (0, j)),
        pl.BlockSpec((3, tf), lambda i, j: (0, nf + j)),
        pl.BlockSpec((1, tf), lambda i, j: (0, j)),
        pl.BlockSpec((1, tf), lambda i, j: (0, nf + j)),
        pl.BlockSpec((tf, d), lambda i, j: (j, 0)),
        pl.BlockSpec((1, d), lambda i, j: (0, 0)),
    ]
    return pl.pallas_call(
        functools.partial(_ffn_kernel, final_norm=final_norm),
        grid=(nt, nf),
        in_specs=in_specs,
        out_specs=pl.BlockSpec((tm, d), lambda i, j: (i, 0)),
        out_shape=jax.ShapeDtypeStruct((s, d), F32),
        scratch_shapes=[
            pltpu.VMEM((BF16_ROWS + tm, d), BF16),
        ] + [
            pltpu.VMEM((tm + 2 * BF16_ROWS, FFN_SUB), F32),
            pltpu.VMEM((tm + 2 * BF16_ROWS, FFN_SUB), F32),
            pltpu.VMEM((tm, FFN_SUB), BF16),
        ] * n_sub,
        compiler_params=pltpu.CompilerParams(
            dimension_semantics=("arbitrary", "arbitrary"),
            vmem_limit_bytes=VMEM_LIMIT_BYTES),
        name="ffn",
    )(h, h, h, g2, w_up, w_up, conv_w, conv_w, conv_b, conv_b, w_down, gf)


def _rope_tables(seq_len):
    half = HEAD_DIM // 2
    reps = LANES // half
    inv_freq = jnp.tile(ROPE_THETA ** (-jnp.arange(0, half, dtype=F32) / half), reps)[None, :]
    sign = jnp.tile(jnp.concatenate([-jnp.ones((half,), F32), jnp.ones((half,), F32)]),
                    reps // 2)
    base = (ROPE_SPAN * jnp.arange(seq_len // ROPE_SPAN, dtype=F32))[:, None] * inv_freq
    offs = jnp.arange(ROPE_SPAN, dtype=F32)[:, None] * inv_freq
    cb, sb = jnp.cos(base)[:, None, :], jnp.sin(base)[:, None, :]
    co, so = jnp.cos(offs)[None, :, :], jnp.sin(offs)[None, :, :]
    cos_t = (cb * co - sb * so).reshape(seq_len, LANES)
    sin_t = ((sb * co + cb * so) * sign).reshape(seq_len, LANES)
    return cos_t, sin_t


def kernel(x, norm_mix_g, w_in, b_gate, conv_a_w, w_out_a, sink_logits, w_o_attn, w_mix_out,
           norm_ffn_g, ffn_w_up, ffn_conv_w, ffn_conv_b, ffn_w_down, norm_final_g):
    b, s, d = x.shape
    depth = w_in.shape[0]
    cos_t, sin_t = _rope_tables(s)
    outs = []
    for bi in range(b):
        h = x[bi]
        for l in range(depth):
            z, (w_oa, w_o, w_mix) = _in_proj(
                h, norm_mix_g[l][None, :], w_in[l].astype(BF16),
                (w_out_a[l], w_o_attn[l], w_mix_out[l]))
            h, (w_up, w_down) = _mixer(
                h, z, cos_t, sin_t, sink_logits[l], conv_a_w[l], b_gate[l][None, :],
                w_oa, w_o, w_mix, (ffn_w_up[l], ffn_w_down[l]))
            h = _ffn(h, norm_ffn_g[l][None, :], w_up, ffn_conv_w[l], ffn_conv_b[l][None, :],
                     w_down, norm_final_g[None, :], final_norm=(l == depth - 1))
        outs.append(h[None])
    return outs[0] if b == 1 else jnp.concatenate(outs, axis=0)
```
